```python
import math
import jax
import jax.numpy as jnp
from jax import lax
import numpy as np

D_MODEL = 1024
BATCH = 32
SEQ = 2048
DEPTH = 2

HEAD_DIM = 64
SSD_WIDTH = 3 * D_MODEL // 8
LRU_WIDTH = D_MODEL // 4
FOX_WIDTH = D_MODEL - SSD_WIDTH - LRU_WIDTH
SSD_HEADS = SSD_WIDTH // HEAD_DIM
SSD_GROUPS = 2
SSD_STATE = 128
SSD_CONV = 4
SSD_CONV_DIM = SSD_WIDTH + 2 * SSD_GROUPS * SSD_STATE
SSD_CHUNK = 128
LRU_BLOCKS = 4
LRU_BLOCK = LRU_WIDTH // LRU_BLOCKS
LRU_CONV = 4
LRU_C = 8.0
FOX_HEADS = FOX_WIDTH // HEAD_DIM
Q_BLOCK = 128
D_FF = ((8 * D_MODEL + 3 * 256 - 1) // (3 * 256)) * 256
PLE_DIM = 256
IN_SPLITS = (SSD_WIDTH, SSD_CONV_DIM, SSD_HEADS, LRU_WIDTH, LRU_WIDTH,
             FOX_WIDTH, FOX_WIDTH, FOX_WIDTH, FOX_HEADS)
IN_COLS = sum(IN_SPLITS)
EPS = 1e-6

kernel_name = "hymba_style_ssd_rglru_fox_trunk"


def _rmsnorm(x, g):
    xf = x.astype(jnp.float32)
    y = xf * lax.rsqrt(jnp.mean(xf * xf, axis=-1, keepdims=True) + EPS)
    return (y * g.astype(jnp.float32)).astype(x.dtype)


def _causal_dwconv(x, w, b):
    k, c = w.shape
    y = lax.conv_general_dilated(
        x, w[:, None, :].astype(x.dtype), window_strides=(1,),
        padding=[(k - 1, 0)], dimension_numbers=("NWC", "WIO", "NWC"),
        feature_group_count=c)
    return y + b.astype(x.dtype)


def _ssd(xs, dt, a, bm, cm, d_skip):
    b, s, h, p = xs.shape
    g, n = bm.shape[2], bm.shape[3]
    nc = s // SSD_CHUNK
    rep = h // g
    bh = jnp.repeat(bm, rep, axis=2)
    chh = jnp.repeat(cm, rep, axis=2)
    xdt = xs * dt[..., None]
    adt = a * dt
    chunk = lambda t: t.reshape((b, nc, SSD_CHUNK) + t.shape[2:])
    xc, bc, cc, ac = chunk(xdt), chunk(bh), chunk(chh), chunk(adt)
    acs = jnp.cumsum(ac, axis=2)
    seg = acs[:, :, :, None, :] - acs[:, :, None, :, :]
    causal = jnp.tril(jnp.ones((SSD_CHUNK, SSD_CHUNK), bool))
    lmat = jnp.exp(jnp.where(causal[None, None, :, :, None], seg, -jnp.inf))
    scores = jnp.einsum("bclhn,bcshn->bclsh", cc, bc) * lmat
    y_diag = jnp.einsum("bclsh,bcshp->bclhp", scores, xc)
    decay_s = jnp.exp(acs[:, :, -1:, :] - acs)
    states = jnp.einsum("bclhn,bclh,bclhp->bchpn", bc, decay_s, xc)
    chunk_decay = jnp.exp(acs[:, :, -1, :])

    def step(carry, inp):
        st, dec = inp
        return carry * dec[..., None, None] + st, carry

    init = jnp.zeros((b, h, p, n), states.dtype)
    _, prev = lax.scan(step, init, (jnp.moveaxis(states, 1, 0),
                                    jnp.moveaxis(chunk_decay, 1, 0).astype(states.dtype)))
    prev = jnp.moveaxis(prev, 0, 1)
    y_off = jnp.einsum("bclhn,bchpn,bclh->bclhp", cc, prev, jnp.exp(acs))
    y = (y_diag + y_off).reshape(b, s, h, p) + xs * d_skip[:, None]
    return y.astype(xs.dtype)


def _rglru(x, w_a, b_a, w_x, b_x, lam):
    b, s, w = x.shape
    xb = x.reshape(b, s, LRU_BLOCKS, LRU_BLOCK)
    r = jax.nn.sigmoid(jnp.einsum("bsgi,gij->bsgj", xb, w_a).reshape(b, s, w) + b_a)
    i = jax.nn.sigmoid(jnp.einsum("bsgi,gij->bsgj", xb, w_x).reshape(b, s, w) + b_x)
    log_a = -LRU_C * r.astype(jnp.float32) * jax.nn.softplus(-lam.astype(jnp.float32))
    a = jnp.exp(log_a)
    mult = jnp.sqrt(-jnp.expm1(2.0 * log_a))
    u = mult * (i * x).astype(jnp.float32)

    def comb(lhs, rhs):
        a1, b1 = lhs
        a2, b2 = rhs
        return a1 * a2, a2 * b1 + b2

    _, hseq = lax.associative_scan(comb, (a, u), axis=1)
    return hseq.astype(x.dtype)


def _forgetting_attention(q, k, v, log_f):
    s, e = q.shape[1], q.shape[3]
    cum = jnp.cumsum(log_f, axis=-1)
    scale = e ** -0.5
    outs = []
    for blk in range(s // Q_BLOCK):
        q0, q1 = blk * Q_BLOCK, (blk + 1) * Q_BLOCK
        logits = jnp.einsum("bqhe,bkhe->bhqk", q[:, q0:q1], k[:, :q1]).astype(jnp.float32) * scale
        logits = logits + cum[:, :, q0:q1, None] - cum[:, :, None, :q1]
        mask = (q0 + jnp.arange(Q_BLOCK))[:, None] >= jnp.arange(q1)[None, :]
        logits = jnp.where(mask[None, None], logits, -jnp.inf)
        probs = jax.nn.softmax(logits, axis=-1).astype(v.dtype)
        outs.append(jnp.einsum("bhqk,bkhe->bqhe", probs, v[:, :q1]))
    return jnp.concatenate(outs, axis=1)


def _mixer(u, w_in, ssd_conv_w, ssd_conv_b, ssd_dt_bias, ssd_a_log, ssd_d, ssd_norm_g,
           lru_conv_w, lru_conv_b, lru_w_a, lru_b_a, lru_w_x, lru_b_x, lru_lambda, lru_norm_g,
           fox_b_f, fox_norm_g, w_out):
    b, s, _ = u.shape
    proj = u @ w_in
    offs = [int(o) for o in np.cumsum(IN_SPLITS)[:-1]]
    (z, xbc, dt_raw, lru_x, lru_gate, fq, fk, fv, f_raw) = jnp.split(proj, offs, axis=-1)

    xbc = jax.nn.silu(_causal_dwconv(xbc, ssd_conv_w, ssd_conv_b))
    xs, bm, cm = jnp.split(xbc, [SSD_WIDTH, SSD_WIDTH + SSD_GROUPS * SSD_STATE], axis=-1)
    xs = xs.reshape(b, s, SSD_HEADS, HEAD_DIM)
    bm = bm.reshape(b, s, SSD_GROUPS, SSD_STATE)
    cm = cm.reshape(b, s, SSD_GROUPS, SSD_STATE)
    dt = jax.nn.softplus(dt_raw.astype(jnp.float32) + ssd_dt_bias.astype(jnp.float32))
    a = -jnp.exp(ssd_a_log.astype(jnp.float32))
    y_ssd = _ssd(xs, dt, a, bm, cm, ssd_d).reshape(b, s, SSD_WIDTH)
    y_ssd = _rmsnorm(y_ssd * jax.nn.silu(z), ssd_norm_g)

    xl = _causal_dwconv(lru_x, lru_conv_w, lru_conv_b)
    hl = _rglru(xl, lru_w_a, lru_b_a, lru_w_x, lru_b_x, lru_lambda)
    y_lru = _rmsnorm(hl * jax.nn.gelu(lru_gate), lru_norm_g)

    q = fq.reshape(b, s, FOX_HEADS, HEAD_DIM)
    k = fk.reshape(b, s, FOX_HEADS, HEAD_DIM)
    v = fv.reshape(b, s, FOX_HEADS, HEAD_DIM)
    log_f = jnp.transpose(jax.nn.log_sigmoid(f_raw.astype(jnp.float32) + fox_b_f.astype(jnp.float32)), (0, 2, 1))
    y_fox = _forgetting_attention(q, k, v, log_f).reshape(b, s, FOX_WIDTH)
    y_fox = _rmsnorm(y_fox, fox_norm_g)

    return jnp.concatenate([y_ssd, y_lru, y_fox], axis=-1) @ w_out


def setup_inputs(seed: int = 0) -> dict:
    key = jax.random.key(seed)
    ks = jax.random.split(key, 32)
    f32 = jnp.float32
    L = DEPTH
    nrm = lambda k, shape, scale: jax.random.normal(k, shape, f32) * scale
    gain = lambda k, shape: 1.0 + 0.02 * jax.random.normal(k, shape, f32)
    dt0 = jnp.exp(jax.random.uniform(ks[6], (L, SSD_HEADS), f32, math.log(1e-3), math.log(1e-1)))
    lam_s = jax.random.uniform(ks[16], (L, LRU_WIDTH), f32, 0.9, 0.999) ** (1.0 / LRU_C)
    return {
        "x": jax.random.normal(ks[0], (BATCH, SEQ, D_MODEL), f32),
        "p": jax.random.normal(ks[1], (L, BATCH, SEQ, PLE_DIM), f32),
        "norm1_g": gain(ks[2], (L, D_MODEL)),
        "w_in": nrm(ks[3], (L, D_MODEL, IN_COLS), D_MODEL ** -0.5),
        "ssd_conv_w": nrm(ks[4], (L, SSD_CONV, SSD_CONV_DIM), SSD_CONV ** -0.5),
        "ssd_conv_b": nrm(ks[5], (L, SSD_CONV_DIM), 0.02),
        "ssd_dt_bias": dt0 + jnp.log(-jnp.expm1(-dt0)),
        "ssd_a_log": jnp.log(jax.random.uniform(ks[7], (L, SSD_HEADS), f32, 1.0, 16.0)),
        "ssd_d": gain(ks[8], (L, SSD_HEADS)),
        "ssd_norm_g": gain(ks[9], (L, SSD_WIDTH)),
        "lru_conv_w": nrm(ks[10], (L, LRU_CONV, LRU_WIDTH), LRU_CONV ** -0.5),
        "lru_conv_b": nrm(ks[11], (L, LRU_WIDTH), 0.02),
        "lru_w_a": nrm(ks[12], (L, LRU_BLOCKS, LRU_BLOCK, LRU_BLOCK), LRU_BLOCK ** -0.5),
        "lru_b_a": nrm(ks[13], (L, LRU_WIDTH), 0.02),
        "lru_w_x": nrm(ks[14], (L, LRU_BLOCKS, LRU_BLOCK, LRU_BLOCK), LRU_BLOCK ** -0.5),
        "lru_b_x": nrm(ks[15], (L, LRU_WIDTH), 0.02),
        "lru_lambda": jnp.log(lam_s) - jnp.log1p(-lam_s),
        "lru_norm_g": gain(ks[17], (L, LRU_WIDTH)),
        "fox_b_f": 3.0 + nrm(ks[18], (L, FOX_HEADS), 0.1),
        "fox_norm_g": gain(ks[19], (L, FOX_WIDTH)),
        "w_out": nrm(ks[20], (L, D_MODEL, D_MODEL), D_MODEL ** -0.5),
        "norm2_g": gain(ks[21], (L, D_MODEL)),
        "w_gate": nrm(ks[22], (L, D_MODEL, D_FF), D_MODEL ** -0.5),
        "w_up": nrm(ks[23], (L, D_MODEL, D_FF), D_MODEL ** -0.5),
        "w_down": nrm(ks[24], (L, D_FF, D_MODEL), D_FF ** -0.5),
        "norm3_g": gain(ks[25], (L, D_MODEL)),
        "w_ple_gate": nrm(ks[26], (L, D_MODEL, D_MODEL), D_MODEL ** -0.5),
        "b_ple_gate": nrm(ks[27], (L, D_MODEL), 0.02),
        "w_ple_proj": nrm(ks[28], (L, PLE_DIM, D_MODEL), PLE_DIM ** -0.5),
        "final_norm_g": gain(ks[29], (D_MODEL,)),
    }


def reference(x, p, norm1_g, w_in, ssd_conv_w, ssd_conv_b, ssd_dt_bias, ssd_a_log, ssd_d,
              ssd_norm_g, lru_conv_w, lru_conv_b, lru_w_a, lru_b_a, lru_w_x, lru_b_x,
              lru_lambda, lru_norm_g, fox_b_f, fox_norm_g, w_out, norm2_g, w_gate, w_up,
              w_down, norm3_g, w_ple_gate, b_ple_gate, w_ple_proj, final_norm_g):
    h = x
    for i in range(DEPTH):
        u = _rmsnorm(h, norm1_g[i])
        h = h + _mixer(u, w_in[i], ssd_conv_w[i], ssd_conv_b[i], ssd_dt_bias[i], ssd_a_log[i],
                       ssd_d[i], ssd_norm_g[i], lru_conv_w[i], lru_conv_b[i], lru_w_a[i],
                       lru_b_a[i], lru_w_x[i], lru_b_x[i], lru_lambda[i], lru_norm_g[i],
                       fox_b_f[i], fox_norm_g[i], w_out[i])
        u = _rmsnorm(h, norm2_g[i])
        h = h + (jax.nn.silu(u @ w_gate[i]) * (u @ w_up[i])) @ w_down[i]
        u = _rmsnorm(h, norm3_g[i])
        gate = jax.nn.sigmoid(u @ w_ple_gate[i] + b_ple_gate[i])
        h = h + gate * (p[i] @ w_ple_proj[i])
    return _rmsnorm(h, final_norm_g)
```

```python
import functools

import jax
import jax.numpy as jnp
from jax import lax
from jax.experimental import pallas as pl
from jax.experimental.pallas import tpu as pltpu

F32 = jnp.float32
BF16 = jnp.bfloat16

EPS = 1e-6
HEAD_DIM = 64
LANES = 128
SSD_CHUNK = 128
SSD_CONV = 4
LRU_CONV = 4
LRU_C = 8.0
SMALL_DT_COL = 0
SMALL_F_COL = 8
VMEM_LIMIT_BYTES = 56 * 1024 * 1024


def _dot(a, b):
    return jnp.dot(a, b, preferred_element_type=F32)


def _dot_nt(a, b):
    return lax.dot_general(a, b, (((1,), (1,)), ((), ())), preferred_element_type=F32)


def _rms(x, g):
    ms = jnp.mean(x * x, axis=-1, keepdims=True)
    return x * lax.rsqrt(ms + EPS) * g


def _sigmoid(x):
    return 1.0 / (1.0 + jnp.exp(-x))


def _softplus(x):
    return jnp.maximum(x, 0.0) + jnp.log1p(jnp.exp(-jnp.abs(x)))


def _split3(x):
    hi = x.astype(BF16)
    r = x - hi.astype(F32)
    mid = r.astype(BF16)
    lo = (r - mid.astype(F32)).astype(BF16)
    return hi, mid, lo


def _exact_right(x, m):
    hi, mid, lo = _split3(x)
    return _dot(hi, m) + _dot(mid, m) + _dot(lo, m)


def _exact_left(m, x):
    hi, mid, lo = _split3(x)
    return _dot(m, hi) + _dot(m, mid) + _dot(m, lo)


def _iota(shape, axis):
    return lax.broadcasted_iota(jnp.int32, shape, axis)


def _causal_conv(ref, r0, rows, col0, cols, w_ref, b_ref, chunk_idx):
    cur = ref[pl.ds(r0, rows), col0:col0 + cols]
    p0 = pl.multiple_of(jnp.maximum(r0 - 8, 0), 8)
    prev = ref[pl.ds(p0, 8), col0:col0 + cols]
    prev = jnp.where(chunk_idx > 0, prev, 0.0)
    ext = jnp.concatenate([prev, cur], axis=0)
    k = w_ref.shape[0]
    acc = cur * w_ref[k - 1:k, :] + b_ref[...]
    for j in range(1, k):
        acc = acc + pltpu.roll(ext, j, 0)[8:, :] * w_ref[k - 1 - j:k - j, :]
    return acc


def _in_proj_kernel(h_ref, g_ref, w_ref, *out_refs, slices):
    u = _rms(h_ref[...], g_ref[...]).astype(BF16)
    for o_ref, (a, b) in zip(out_refs, slices):
        o_ref[...] = _dot(u, w_ref[:, a:b]).astype(o_ref.dtype)


def _in_proj(h, g, w, widths, dtypes, tm):
    t, d = h.shape
    offs = [0]
    for wd in widths:
        offs.append(offs[-1] + wd)
    slices = tuple((offs[i], offs[i + 1]) for i in range(len(widths)))
    const = lambda i: (0, 0)
    return pl.pallas_call(
        functools.partial(_in_proj_kernel, slices=slices),
        grid=(t // tm,),
        in_specs=[pl.BlockSpec((tm, d), lambda i: (i, 0)),
                  pl.BlockSpec((1, d), const),
                  pl.BlockSpec(w.shape, const, pipeline_mode=pl.Buffered(1))],
        out_specs=[pl.BlockSpec((tm, wd), lambda i: (i, 0)) for wd in widths],
        out_shape=[jax.ShapeDtypeStruct((t, wd), dt) for wd, dt in zip(widths, dtypes)],
        compiler_params=pltpu.CompilerParams(dimension_semantics=("arbitrary",),
                                             vmem_limit_bytes=VMEM_LIMIT_BYTES),
        name="in_proj",
    )(h, g, w)


def _ssd_kernel(z_ref, xbc_ref, small_ref, cw_ref, cb_ref, dtb_ref, alog_ref, dsk_ref, g_ref,
                o_ref, st_ref, *, heads, groups, state):
    seq = xbc_ref.shape[0]
    L = SSD_CHUNK
    width = heads * HEAD_DIM
    pairs = width // LANES
    hpg = heads // groups
    causal = _iota((L, L), 0) >= _iota((L, L), 1)
    tril = causal.astype(BF16)
    expand = (_iota((LANES, width), 0) + SMALL_DT_COL == _iota((LANES, width), 1) // HEAD_DIM).astype(BF16)
    head_lane = (_iota((1, LANES), 1) >= SMALL_DT_COL) & (_iota((1, LANES), 1) < SMALL_DT_COL + heads)
    first_half = _iota((L, LANES), 1) < HEAD_DIM
    bd_rows = _iota((2 * state, LANES), 0) < state
    bd_mask = bd_rows == (_iota((2 * state, LANES), 1) < HEAD_DIM)
    a_row = jnp.where(head_lane, -jnp.exp(alog_ref[...]), 0.0)

    st_ref[...] = jnp.zeros_like(st_ref)

    def chunk(c, carry):
        r0 = pl.multiple_of(c * L, L)
        xc = _causal_conv(xbc_ref, r0, L, 0, xbc_ref.shape[1], cw_ref, cb_ref, c)
        xc = xc * _sigmoid(xc)
        xs = xc[:, :width]
        bm = xc[:, width:width + groups * state]
        cm = xc[:, width + groups * state:]
        sm = small_ref[pl.ds(r0, L), :]
        dt = jnp.where(head_lane, _softplus(sm + dtb_ref[...]), 0.0)
        acs = _exact_left(tril, dt * a_row)
        acs_t = acs.T
        dt_e = _exact_right(dt, expand)
        acs_e = _exact_right(acs, expand)
        last_e = acs_e[L - 1:L, :]
        xdt = xs * dt_e
        xdw = xdt * jnp.exp(last_e - acs_e)
        eacs = jnp.exp(acs_e)
        cdec = jnp.exp(last_e)
        bm_b = bm.astype(BF16)
        cm_b = cm.astype(BF16)
        cb = [_dot_nt(cm_b[:, g * state:(g + 1) * state], bm_b[:, g * state:(g + 1) * state])
              for g in range(groups)]
        sc = []
        for h in range(heads):
            seg = acs[:, h:h + 1] - acs_t[h:h + 1, :]
            lm = jnp.exp(jnp.where(causal, seg, -jnp.inf))
            sc.append((cb[h // hpg] * lm).astype(BF16))
        ys = []
        for j in range(pairs):
            ha, hb = 2 * j, 2 * j + 1
            ga, gb = ha // hpg, hb // hpg
            cs = slice(j * LANES, (j + 1) * LANES)
            xp = xdt[:, cs]
            x_bd = jnp.concatenate([jnp.where(first_half, xp, 0.0), jnp.where(first_half, 0.0, xp)],
                                   axis=0).astype(BF16)
            yd = _dot(jnp.concatenate([sc[ha], sc[hb]], axis=1), x_bd)
            cm_p = jnp.concatenate([cm_b[:, ga * state:(ga + 1) * state],
                                    cm_b[:, gb * state:(gb + 1) * state]], axis=1)
            st = st_ref[j]
            yo = _dot(cm_p, st.astype(BF16)) * eacs[:, cs]
            bm_pt = jnp.concatenate([bm[:, ga * state:(ga + 1) * state],
                                     bm[:, gb * state:(gb + 1) * state]], axis=1).T.astype(BF16)
            new = _dot(bm_pt, xdw[:, cs].astype(BF16))
            st_ref[j] = st * cdec[:, cs] + jnp.where(bd_mask, new, 0.0)
            ys.append(yd + yo + xs[:, cs] * dsk_ref[:, cs])
        y = jnp.concatenate(ys, axis=1)
        zz = z_ref[pl.ds(r0, L), :]
        y = y * (zz * _sigmoid(zz))
        o_ref[pl.ds(r0, L), :] = _rms(y, g_ref[...])
        return carry

    lax.fori_loop(0, seq // L, chunk, 0)


def _ssd(z, xbc, small, cw, cb, dtb, alog, dsk, g, *, batch, seq, heads, groups, state):
    width = heads * HEAD_DIM
    const = lambda b: (0, 0)
    row = lambda b: (b, 0)
    return pl.pallas_call(
        functools.partial(_ssd_kernel, heads=heads, groups=groups, state=state),
        grid=(batch,),
        in_specs=[pl.BlockSpec((seq, width), row),
                  pl.BlockSpec((seq, xbc.shape[1]), row),
                  pl.BlockSpec((seq, LANES), row),
                  pl.BlockSpec(cw.shape, const), pl.BlockSpec(cb.shape, const),
                  pl.BlockSpec(dtb.shape, const), pl.BlockSpec(alog.shape, const),
                  pl.BlockSpec(dsk.shape, const), pl.BlockSpec(g.shape, const)],
        out_specs=pl.BlockSpec((seq, width), row),
        out_shape=jax.ShapeDtypeStruct((batch * seq, width), F32),
        scratch_shapes=[pltpu.VMEM((width // LANES, 2 * state, LANES), F32)],
        compiler_params=pltpu.CompilerParams(dimension_semantics=("arbitrary",),
                                             vmem_limit_bytes=VMEM_LIMIT_BYTES),
        name="ssd",
    )(z, xbc, small, cw, cb, dtb, alog, dsk, g)


def _gelu_tanh(x):
    return 0.5 * x * (1.0 + jnp.tanh(0.7978845608028654 * (x + 0.044715 * (x * x * x))))


def _lru_kernel(x_ref, cw_ref, cb_ref, wa_ref, ba_ref, wx_ref, bx_ref, lam_ref, g_ref, o_ref, *, chunk):
    seq = x_ref.shape[0]
    w = o_ref.shape[1]
    sp = _softplus(-lam_ref[...])
    rowi = _iota((chunk, w), 0)

    def body(c, hprev):
        r0 = pl.multiple_of(c * chunk, chunk)
        xl = _causal_conv(x_ref, r0, chunk, 0, w, cw_ref, cb_ref, c)
        gate = x_ref[pl.ds(r0, chunk), w:2 * w]
        xb = xl.astype(BF16)
        r = _sigmoid(_dot(xb, wa_ref[...]) + ba_ref[...])
        i = _sigmoid(_dot(xb, wx_ref[...]) + bx_ref[...])
        log_a = (-LRU_C) * r * sp
        a = jnp.exp(log_a)
        u = jnp.sqrt(-jnp.tanh(log_a) * (a * a + 1.0)) * (i * xl)
        d = 1
        while d < chunk:
            keep = rowi >= d
            a_sh = jnp.where(keep, pltpu.roll(a, d, 0), 1.0)
            u_sh = jnp.where(keep, pltpu.roll(u, d, 0), 0.0)
            u = a * u_sh + u
            a = a * a_sh
            d *= 2
        h = a * hprev + u
        o_ref[pl.ds(r0, chunk), :] = _rms(h * _gelu_tanh(gate), g_ref[...])
        return h[chunk - 1:chunk, :]

    lax.fori_loop(0, seq // chunk, body, jnp.zeros((1, w), F32))


def _lru(x, cw, cb, wa, ba, wx, bx, lam, g, *, batch, seq, chunk=256):
    w = cw.shape[1]
    const = lambda b: (0, 0)
    row = lambda b: (b, 0)
    return pl.pallas_call(
        functools.partial(_lru_kernel, chunk=chunk),
        grid=(batch,),
        in_specs=[pl.BlockSpec((seq, 2 * w), row),
                  pl.BlockSpec(cw.shape, const), pl.BlockSpec(cb.shape, const),
                  pl.BlockSpec(wa.shape, const), pl.BlockSpec(ba.shape, const),
                  pl.BlockSpec(wx.shape, const), pl.BlockSpec(bx.shape, const),
                  pl.BlockSpec(lam.shape, const), pl.BlockSpec(g.shape, const)],
        out_specs=pl.BlockSpec((seq, w), row),
        out_shape=jax.ShapeDtypeStruct((batch * seq, w), F32),
        compiler_params=pltpu.CompilerParams(dimension_semantics=("arbitrary",),
                                             vmem_limit_bytes=VMEM_LIMIT_BYTES),
        name="rglru",
    )(x, cw, cb, wa, ba, wx, bx, lam, g)


def _fox_kernel(qkv_ref, small_ref, bf_ref, g_ref, o_ref, cum_ref, cumt_ref, *, heads, tq):
    seq = qkv_ref.shape[0]
    width = heads * HEAD_DIM
    pairs = width // LANES
    tk = tq
    qi = pl.program_id(1)
    scale = HEAD_DIM ** -0.5

    @pl.when(qi == 0)
    def _():
        tril = (_iota((LANES, LANES), 0) >= _iota((LANES, LANES), 1)).astype(BF16)
        f_lane = (_iota((1, LANES), 1) >= SMALL_F_COL) & (_iota((1, LANES), 1) < SMALL_F_COL + heads)

        def body(c, carry):
            r0 = pl.multiple_of(c * LANES, LANES)
            sm = small_ref[pl.ds(r0, LANES), :]
            log_f = jnp.where(f_lane, -_softplus(-(sm + bf_ref[...])), 0.0)
            cs = _exact_left(tril, log_f) + carry
            cum_ref[pl.ds(r0, LANES), :] = cs
            cumt_ref[c] = cs.T
            return cs[LANES - 1:LANES, :]

        lax.fori_loop(0, seq // LANES, body, jnp.zeros((1, LANES), F32))

    q0 = pl.multiple_of(qi * tq, tq)
    first_half = _iota((tk, LANES), 1) < HEAD_DIM
    first_half_q = _iota((tq, LANES), 1) < HEAD_DIM
    cum_q = cum_ref[pl.ds(q0, tq), :]
    rel = _iota((tq, tk), 0) - _iota((tq, tk), 1)
    zeros_k = jnp.zeros((tk, LANES), BF16)

    outs = []
    for j in range(pairs):
        ca = SMALL_F_COL + 2 * j
        q2 = qkv_ref[pl.ds(q0, tq), j * LANES:(j + 1) * LANES]
        col_a = cum_q[:, ca:ca + 1]
        col_b = cum_q[:, ca + 1:ca + 2]

        def kv_step(kb, carry, j=j, ca=ca, q2=q2, col_a=col_a, col_b=col_b):
            m_a, m_b, l_a, l_b, acc = carry
            k0 = pl.multiple_of(kb * tk, tk)
            k2 = qkv_ref[pl.ds(k0, tk), width + j * LANES:width + (j + 1) * LANES]
            v2 = qkv_ref[pl.ds(k0, tk), 2 * width + j * LANES:2 * width + (j + 1) * LANES]
            k_bd = jnp.concatenate([jnp.where(first_half, k2, zeros_k),
                                    jnp.where(first_half, zeros_k, k2)], axis=0)
            v_bd = jnp.concatenate([jnp.where(first_half, v2, zeros_k),
                                    jnp.where(first_half, zeros_k, v2)], axis=0)
            s = _dot_nt(q2, k_bd) * scale
            ct = cumt_ref[kb]
            visible = rel >= (kb - qi) * tk
            s_a = jnp.where(visible, s[:, :tk] + col_a - ct[ca:ca + 1, :], -jnp.inf)
            s_b = jnp.where(visible, s[:, tk:] + col_b - ct[ca + 1:ca + 2, :], -jnp.inf)
            mn_a = jnp.maximum(m_a, jnp.max(s_a, axis=-1, keepdims=True))
            mn_b = jnp.maximum(m_b, jnp.max(s_b, axis=-1, keepdims=True))
            p_a = jnp.exp(s_a - mn_a)
            p_b = jnp.exp(s_b - mn_b)
            al_a = jnp.exp(m_a - mn_a)
            al_b = jnp.exp(m_b - mn_b)
            l_a = al_a * l_a + jnp.sum(p_a, axis=-1, keepdims=True)
            l_b = al_b * l_b + jnp.sum(p_b, axis=-1, keepdims=True)
            p = jnp.concatenate([p_a, p_b], axis=1).astype(BF16)
            acc = acc * jnp.where(first_half_q, al_a, al_b) + _dot(p, v_bd)
            return mn_a, mn_b, l_a, l_b, acc

        neg = jnp.full((tq, 1), -jnp.inf, F32)
        zero = jnp.zeros((tq, 1), F32)
        m_a, m_b, l_a, l_b, acc = lax.fori_loop(
            0, qi + 1, kv_step, (neg, neg, zero, zero, jnp.zeros((tq, LANES), F32)))
        outs.append(acc / jnp.where(first_half_q, l_a, l_b))
    y = jnp.concatenate(outs, axis=1)
    o_ref[...] = _rms(y, g_ref[...])


def _fox(qkv, small, bf, g, *, batch, seq, heads, tq=128):
    width = heads * HEAD_DIM
    const = lambda b, i: (0, 0)
    return pl.pallas_call(
        functools.partial(_fox_kernel, heads=heads, tq=tq),
        grid=(batch, seq // tq),
        in_specs=[pl.BlockSpec((seq, 3 * width), lambda b, i: (b, 0)),
                  pl.BlockSpec((seq, LANES), lambda b, i: (b, 0)),
                  pl.BlockSpec(bf.shape, const), pl.BlockSpec(g.shape, const)],
        out_specs=pl.BlockSpec((tq, width), lambda b, i: (b * (seq // tq) + i, 0)),
        out_shape=jax.ShapeDtypeStruct((batch * seq, width), F32),
        scratch_shapes=[pltpu.VMEM((seq, LANES), F32),
                        pltpu.VMEM((seq // LANES, LANES, LANES), F32)],
        compiler_params=pltpu.CompilerParams(dimension_semantics=("arbitrary", "arbitrary"),
                                             vmem_limit_bytes=VMEM_LIMIT_BYTES),
        name="fox",
    )(qkv, small, bf, g)


def _mixer_kernel(h_ref, ys_ref, yl_ref, yf_ref, p_ref, wo_ref, g2_ref, wg_ref, wu_ref, wd_ref,
                  g3_ref, wpg_ref, bpg_ref, wpp_ref, gf_ref, o_ref, *, ff_chunk, final_norm):
    y = jnp.concatenate([ys_ref[...], yl_ref[...], yf_ref[...]], axis=1).astype(BF16)
    h = h_ref[...] + _dot(y, wo_ref[...])
    u = _rms(h, g2_ref[...]).astype(BF16)
    d_ff = wg_ref.shape[1]
    for c0 in range(0, d_ff, ff_chunk):
        gt = _dot(u, wg_ref[:, c0:c0 + ff_chunk])
        up = _dot(u, wu_ref[:, c0:c0 + ff_chunk])
        act = (gt * _sigmoid(gt) * up).astype(BF16)
        h = h + _dot(act, wd_ref[c0:c0 + ff_chunk, :])
    u = _rms(h, g3_ref[...]).astype(BF16)
    gate = _sigmoid(_dot(u, wpg_ref[...]) + bpg_ref[...])
    h = h + gate * _dot(p_ref[...].astype(BF16), wpp_ref[...])
    if final_norm:
        h = _rms(h, gf_ref[...])
    o_ref[...] = h


def _mixer(h, ys, yl, yf, p, wo, g2, wg, wu, wd, g3, wpg, bpg, wpp, gf, *, tm, ff_chunk, final_norm):
    t, d = h.shape
    const = lambda i: (0, 0)
    row = lambda i: (i, 0)
    wspec = lambda a: pl.BlockSpec(a.shape, const, pipeline_mode=pl.Buffered(1))
    vspec = lambda a: pl.BlockSpec(a.shape, const)
    return pl.pallas_call(
        functools.partial(_mixer_kernel, ff_chunk=ff_chunk, final_norm=final_norm),
        grid=(t // tm,),
        in_specs=[pl.BlockSpec((tm, d), row),
                  pl.BlockSpec((tm, ys.shape[1]), row),
                  pl.BlockSpec((tm, yl.shape[1]), row),
                  pl.BlockSpec((tm, yf.shape[1]), row),
                  pl.BlockSpec((tm, p.shape[1]), row),
                  wspec(wo), vspec(g2), wspec(wg), wspec(wu), wspec(wd),
                  vspec(g3), wspec(wpg), vspec(bpg), wspec(wpp), vspec(gf)],
        out_specs=pl.BlockSpec((tm, d), row),
        out_shape=jax.ShapeDtypeStruct((t, d), F32),
        compiler_params=pltpu.CompilerParams(dimension_semantics=("arbitrary",),
                                             vmem_limit_bytes=VMEM_LIMIT_BYTES),
        name="mixer",
    )(h, ys, yl, yf, p, wo, g2, wg, wu, wd, g3, wpg, bpg, wpp, gf)


def _pad_lanes(v, col0):
    out = jnp.zeros((1, LANES), F32)
    return out.at[0, col0:col0 + v.shape[0]].set(v.astype(F32))


def _block_diag(w):
    nb, bw, _ = w.shape
    out = jnp.zeros((nb * bw, nb * bw), w.dtype)
    for g in range(nb):
        out = out.at[g * bw:(g + 1) * bw, g * bw:(g + 1) * bw].set(w[g])
    return out


def kernel(x, p, norm1_g, w_in, ssd_conv_w, ssd_conv_b, ssd_dt_bias, ssd_a_log, ssd_d, ssd_norm_g,
           lru_conv_w, lru_conv_b, lru_w_a, lru_b_a, lru_w_x, lru_b_x, lru_lambda, lru_norm_g,
           fox_b_f, fox_norm_g, w_out, norm2_g, w_gate, w_up, w_down, norm3_g, w_ple_gate,
           b_ple_gate, w_ple_proj, final_norm_g):
    batch, seq, d_model = x.shape
    depth = w_in.shape[0]
    ssd_heads = ssd_dt_bias.shape[1]
    ssd_width = ssd_heads * HEAD_DIM
    conv_dim = ssd_conv_w.shape[2]
    ssd_groups = 2
    ssd_state = (conv_dim - ssd_width) // (2 * ssd_groups)
    lru_width = lru_conv_w.shape[2]
    fox_heads = fox_b_f.shape[1]
    fox_width = fox_heads * HEAD_DIM
    tokens = batch * seq

    o_z = 0
    o_xbc = o_z + ssd_width
    o_dt = o_xbc + conv_dim
    o_lx = o_dt + ssd_heads
    o_q = o_lx + 2 * lru_width
    o_f = o_q + 3 * fox_width
    widths = (ssd_width, conv_dim, 2 * lru_width, 3 * fox_width, LANES)
    dtypes = (F32, F32, F32, BF16, F32)

    h = x.reshape(tokens, d_model)
    row = lambda v: v.reshape(1, -1).astype(F32)
    for i in range(depth):
        w = w_in[i]
        w_small = jnp.zeros((d_model, LANES), w.dtype)
        w_small = w_small.at[:, SMALL_DT_COL:SMALL_DT_COL + ssd_heads].set(w[:, o_dt:o_lx])
        w_small = w_small.at[:, SMALL_F_COL:SMALL_F_COL + fox_heads].set(w[:, o_f:o_f + fox_heads])
        w_perm = jnp.concatenate([w[:, o_z:o_dt], w[:, o_lx:o_f], w_small], axis=1).astype(BF16)

        z, xbc, lru_in, qkv, small = _in_proj(h, row(norm1_g[i]), w_perm, widths, dtypes, tm=512)

        y_ssd = _ssd(z, xbc, small, ssd_conv_w[i], row(ssd_conv_b[i]),
                     _pad_lanes(ssd_dt_bias[i], SMALL_DT_COL), _pad_lanes(ssd_a_log[i], SMALL_DT_COL),
                     row(jnp.repeat(ssd_d[i], HEAD_DIM)), row(ssd_norm_g[i]),
                     batch=batch, seq=seq, heads=ssd_heads, groups=ssd_groups, state=ssd_state)
        y_lru = _lru(lru_in, lru_conv_w[i], row(lru_conv_b[i]),
                     _block_diag(lru_w_a[i]).astype(BF16), row(lru_b_a[i]),
                     _block_diag(lru_w_x[i]).astype(BF16), row(lru_b_x[i]),
                     row(lru_lambda[i]), row(lru_norm_g[i]), batch=batch, seq=seq)
        y_fox = _fox(qkv, small, _pad_lanes(fox_b_f[i], SMALL_F_COL), row(fox_norm_g[i]),
                     batch=batch, seq=seq, heads=fox_heads)
        h = _mixer(h, y_ssd, y_lru, y_fox, p[i].reshape(tokens, -1),
                   w_out[i].astype(BF16), row(norm2_g[i]),
                   w_gate[i].astype(BF16), w_up[i].astype(BF16), w_down[i].astype(BF16),
                   row(norm3_g[i]), w_ple_gate[i].astype(BF16), row(b_ple_gate[i]),
                   w_ple_proj[i].astype(BF16), row(final_norm_g),
                   tm=512, ff_chunk=256, final_norm=(i == depth - 1))
    return h.reshape(batch, seq, d_model)
```

```python
import functools
import math

import jax
import jax.numpy as jnp
import numpy as np
from jax import lax
from jax.experimental import pallas as pl
from jax.experimental.pallas import tpu as pltpu

F32 = jnp.float32
BF16 = jnp.bfloat16

EPS = 1e-6
HEAD_DIM = 64
LANES = 128
BF16_SUBLANES = 16
SSD_CHUNK = 128
SSD_GROUPS = 2
LRU_C = 8.0
SMALL_DT_COL = 0
SMALL_F_COL = 8
VMEM_LIMIT_BYTES = 56 * 1024 * 1024


def _dot(a, b):
    return jnp.dot(a, b, preferred_element_type=F32)


def _dot_nt(a, b):
    return lax.dot_general(a, b, (((1,), (1,)), ((), ())), preferred_element_type=F32)


def _rms(x, g):
    ms = jnp.mean(x * x, axis=-1, keepdims=True)
    return x * lax.rsqrt(ms + EPS) * g


def _sigmoid(x):
    return 1.0 / (1.0 + jnp.exp(-x))


def _softplus(x):
    return jnp.maximum(x, 0.0) + jnp.log1p(jnp.exp(-jnp.abs(x)))


def _split3(x):
    hi = x.astype(BF16)
    r = x - hi.astype(F32)
    mid = r.astype(BF16)
    lo = (r - mid.astype(F32)).astype(BF16)
    return hi, mid, lo


def _exact_right(x, m):
    hi, mid, lo = _split3(x)
    return _dot(hi, m) + _dot(mid, m) + _dot(lo, m)


def _exact_left(m, x):
    hi, mid, lo = _split3(x)
    return _dot(m, hi) + _dot(m, mid) + _dot(m, lo)


def _iota(shape, axis):
    return lax.broadcasted_iota(jnp.int32, shape, axis)


def _causal_conv(ref, r0, rows, cols, w_ref, b_ref, chunk_idx):
    cur = ref[pl.ds(r0, rows), 0:cols]
    p0 = pl.multiple_of(jnp.maximum(r0 - 8, 0), 8)
    prev = ref[pl.ds(p0, 8), 0:cols]
    prev = jnp.where(chunk_idx > 0, prev, 0.0)
    ext = jnp.concatenate([prev, cur], axis=0)
    k = w_ref.shape[0]
    acc = cur * w_ref[k - 1:k, :] + b_ref[...]
    for j in range(1, k):
        acc = acc + pltpu.roll(ext, j, 0)[8:, :] * w_ref[k - 1 - j:k - j, :]
    return acc


def _in_proj_kernel(h_ref, g_ref, w_ref, *out_refs, slices):
    u = _rms(h_ref[...], g_ref[...]).astype(BF16)
    for o_ref, (a, b) in zip(out_refs, slices):
        o_ref[...] = _dot(u, w_ref[:, a:b]).astype(o_ref.dtype)


def _in_proj(h, g, w, widths, dtypes, tm):
    t, d = h.shape
    offs = [0]
    for wd in widths:
        offs.append(offs[-1] + wd)
    slices = tuple((offs[i], offs[i + 1]) for i in range(len(widths)))
    const = lambda i: (0, 0)
    return pl.pallas_call(
        functools.partial(_in_proj_kernel, slices=slices),
        grid=(t // tm,),
        in_specs=[pl.BlockSpec((tm, d), lambda i: (i, 0)),
                  pl.BlockSpec((1, d), const),
                  pl.BlockSpec(w.shape, const, pipeline_mode=pl.Buffered(1))],
        out_specs=[pl.BlockSpec((tm, wd), lambda i: (i, 0)) for wd in widths],
        out_shape=[jax.ShapeDtypeStruct((t, wd), dt) for wd, dt in zip(widths, dtypes)],
        compiler_params=pltpu.CompilerParams(dimension_semantics=("arbitrary",),
                                             vmem_limit_bytes=VMEM_LIMIT_BYTES),
        name="in_proj",
    )(h, g, w)


def _ssd_kernel(z_ref, xbc_ref, small_ref, cw_ref, cb_ref, dtb_ref, alog_ref, dsk_ref, g_ref,
                o_ref, st_ref, *, heads, groups, state):
    seq = xbc_ref.shape[0]
    L = SSD_CHUNK
    width = heads * HEAD_DIM
    pairs = width // LANES
    hpg = heads // groups
    causal = _iota((L, L), 0) >= _iota((L, L), 1)
    tril = causal.astype(BF16)
    expand = (_iota((LANES, width), 0) + SMALL_DT_COL == _iota((LANES, width), 1) // HEAD_DIM).astype(BF16)
    head_lane = (_iota((1, LANES), 1) >= SMALL_DT_COL) & (_iota((1, LANES), 1) < SMALL_DT_COL + heads)
    first_half = _iota((L, LANES), 1) < HEAD_DIM
    bd_rows = _iota((2 * state, LANES), 0) < state
    bd_mask = bd_rows == (_iota((2 * state, LANES), 1) < HEAD_DIM)
    a_row = jnp.where(head_lane, -jnp.exp(alog_ref[...]), 0.0)

    st_ref[...] = jnp.zeros_like(st_ref)

    def chunk(c, carry):
        r0 = pl.multiple_of(c * L, L)
        xc = _causal_conv(xbc_ref, r0, L, xbc_ref.shape[1], cw_ref, cb_ref, c)
        xc = xc * _sigmoid(xc)
        xs = xc[:, :width]
        bm = xc[:, width:width + groups * state]
        cm = xc[:, width + groups * state:]
        sm = small_ref[pl.ds(r0, L), :]
        dt = jnp.where(head_lane, _softplus(sm + dtb_ref[...]), 0.0)
        acs = _exact_left(tril, dt * a_row)
        acs_t = acs.T
        dt_e = _exact_right(dt, expand)
        acs_e = _exact_right(acs, expand)
        last_e = acs_e[L - 1:L, :]
        xdt = xs * dt_e
        xdw = xdt * jnp.exp(last_e - acs_e)
        eacs = jnp.exp(acs_e)
        cdec = jnp.exp(last_e)
        bm_b = bm.astype(BF16)
        cm_b = cm.astype(BF16)
        cb = [_dot_nt(cm_b[:, g * state:(g + 1) * state], bm_b[:, g * state:(g + 1) * state])
              for g in range(groups)]
        sc = []
        for h in range(heads):
            seg = acs[:, h:h + 1] - acs_t[h:h + 1, :]
            lm = jnp.exp(jnp.where(causal, seg, -jnp.inf))
            sc.append((cb[h // hpg] * lm).astype(BF16))
        ys = []
        for j in range(pairs):
            ha, hb = 2 * j, 2 * j + 1
            ga, gb = ha // hpg, hb // hpg
            cs = slice(j * LANES, (j + 1) * LANES)
            xp = xdt[:, cs]
            x_bd = jnp.concatenate([jnp.where(first_half, xp, 0.0), jnp.where(first_half, 0.0, xp)],
                                   axis=0).astype(BF16)
            yd = _dot(jnp.concatenate([sc[ha], sc[hb]], axis=1), x_bd)
            cm_p = jnp.concatenate([cm_b[:, ga * state:(ga + 1) * state],
                                    cm_b[:, gb * state:(gb + 1) * state]], axis=1)
            st = st_ref[j]
            yo = _dot(cm_p, st.astype(BF16)) * eacs[:, cs]
            bm_pt = jnp.concatenate([bm[:, ga * state:(ga + 1) * state],
                                     bm[:, gb * state:(gb + 1) * state]], axis=1).T.astype(BF16)
            new = _dot(bm_pt, xdw[:, cs].astype(BF16))
            st_ref[j] = st * cdec[:, cs] + jnp.where(bd_mask, new, 0.0)
            ys.append(yd + yo + xs[:, cs] * dsk_ref[:, cs])
        y = jnp.concatenate(ys, axis=1)
        zz = z_ref[pl.ds(r0, L), :]
        y = y * (zz * _sigmoid(zz))
        o_ref[pl.ds(r0, L), :] = _rms(y, g_ref[...])
        return carry

    lax.fori_loop(0, seq // L, chunk, 0)


def _ssd(z, xbc, small, cw, cb, dtb, alog, dsk, g, *, batch, seq, heads, groups, state):
    width = heads * HEAD_DIM
    const = lambda b: (0, 0)
    row = lambda b: (b, 0)
    return pl.pallas_call(
        functools.partial(_ssd_kernel, heads=heads, groups=groups, state=state),
        grid=(batch,),
        in_specs=[pl.BlockSpec((seq, width), row),
                  pl.BlockSpec((seq, xbc.shape[1]), row),
                  pl.BlockSpec((seq, LANES), row),
                  pl.BlockSpec(cw.shape, const), pl.BlockSpec(cb.shape, const),
                  pl.BlockSpec(dtb.shape, const), pl.BlockSpec(alog.shape, const),
                  pl.BlockSpec(dsk.shape, const), pl.BlockSpec(g.shape, const)],
        out_specs=pl.BlockSpec((seq, width), row),
        out_shape=jax.ShapeDtypeStruct((batch * seq, width), F32),
        scratch_shapes=[pltpu.VMEM((width // LANES, 2 * state, LANES), F32)],
        compiler_params=pltpu.CompilerParams(dimension_semantics=("arbitrary",),
                                             vmem_limit_bytes=VMEM_LIMIT_BYTES),
        name="ssd",
    )(z, xbc, small, cw, cb, dtb, alog, dsk, g)


def _gelu_tanh(x):
    return 0.5 * x * (1.0 + jnp.tanh(0.7978845608028654 * (x + 0.044715 * (x * x * x))))


def _lru_kernel(x_ref, cw_ref, cb_ref, wa_ref, ba_ref, wx_ref, bx_ref, lam_ref, g_ref, o_ref, *, chunk):
    seq = x_ref.shape[0]
    w = o_ref.shape[1]
    sp = _softplus(-lam_ref[...])
    rowi = _iota((chunk, w), 0)

    def body(c, hprev):
        r0 = pl.multiple_of(c * chunk, chunk)
        xl = _causal_conv(x_ref, r0, chunk, w, cw_ref, cb_ref, c)
        gate = x_ref[pl.ds(r0, chunk), w:2 * w]
        xb = xl.astype(BF16)
        r = _sigmoid(_dot(xb, wa_ref[...]) + ba_ref[...])
        i = _sigmoid(_dot(xb, wx_ref[...]) + bx_ref[...])
        log_a = (-LRU_C) * r * sp
        a = jnp.exp(log_a)
        u = jnp.sqrt(-jnp.tanh(log_a) * (a * a + 1.0)) * (i * xl)
        d = 1
        while d < chunk:
            keep = rowi >= d
            a_sh = jnp.where(keep, pltpu.roll(a, d, 0), 1.0)
            u_sh = jnp.where(keep, pltpu.roll(u, d, 0), 0.0)
            u = a * u_sh + u
            a = a * a_sh
            d *= 2
        h = a * hprev + u
        o_ref[pl.ds(r0, chunk), :] = _rms(h * _gelu_tanh(gate), g_ref[...])
        return h[chunk - 1:chunk, :]

    lax.fori_loop(0, seq // chunk, body, jnp.zeros((1, w), F32))


def _lru(x, cw, cb, wa, ba, wx, bx, lam, g, *, batch, seq, chunk=256):
    w = cw.shape[1]
    const = lambda b: (0, 0)
    row = lambda b: (b, 0)
    return pl.pallas_call(
        functools.partial(_lru_kernel, chunk=chunk),
        grid=(batch,),
        in_specs=[pl.BlockSpec((seq, 2 * w), row),
                  pl.BlockSpec(cw.shape, const), pl.BlockSpec(cb.shape, const),
                  pl.BlockSpec(wa.shape, const), pl.BlockSpec(ba.shape, const),
                  pl.BlockSpec(wx.shape, const), pl.BlockSpec(bx.shape, const),
                  pl.BlockSpec(lam.shape, const), pl.BlockSpec(g.shape, const)],
        out_specs=pl.BlockSpec((seq, w), row),
        out_shape=jax.ShapeDtypeStruct((batch * seq, w), F32),
        compiler_params=pltpu.CompilerParams(dimension_semantics=("arbitrary",),
                                             vmem_limit_bytes=VMEM_LIMIT_BYTES),
        name="rglru",
    )(x, cw, cb, wa, ba, wx, bx, lam, g)


FOX_BLOCK = 256
FOX_VT_ROWS = HEAD_DIM + BF16_SUBLANES


def _fox_bias_matrices(heads):
    pairs = heads * HEAD_DIM // LANES
    pm = np.zeros((2, pairs, 3 * LANES, LANES), np.float32)
    ones = np.zeros((2, LANES), np.float32)
    for j in range(pairs):
        for part in range(3):
            ra = part * LANES + SMALL_F_COL + 2 * j
            pm[0, j, ra, HEAD_DIM + part] = 1.0
            pm[0, j, ra + 1, part] = 1.0
            pm[1, j, ra, HEAD_DIM + 3 + part] = -1.0
            pm[1, j, ra + 1, 3 + part] = -1.0
    for base in (0, HEAD_DIM):
        ones[0, base + 3:base + 6] = 1.0
        ones[1, base:base + 3] = 1.0
    return jnp.asarray(pm, BF16), jnp.asarray(ones, F32)


def _fox_kernel(qkv_ref, small_ref, bf_ref, pm_ref, ones_ref, g_ref, o_ref, qaugt_ref, kaug_ref, vt_ref,
                s_ref, *, heads):
    seq = qkv_ref.shape[0]
    width = heads * HEAD_DIM
    pairs = width // LANES
    blk = FOX_BLOCK
    qi = pl.program_id(1)
    scale = HEAD_DIM ** -0.5
    assert math.frexp(scale)[0] == 0.5, "scaling q before its bf16 cast is exact only for a power-of-two scale"

    @pl.when(qi == 0)
    def _():
        tril = (_iota((blk, blk), 0) >= _iota((blk, blk), 1)).astype(BF16)
        f_lane = (_iota((1, LANES), 1) >= SMALL_F_COL) & (_iota((1, LANES), 1) < SMALL_F_COL + heads)
        head_a = _iota((blk, LANES), 1) < HEAD_DIM
        ones_rows = jnp.where(_iota((BF16_SUBLANES, blk), 0) == 0, 1.0, 0.0)

        def body(c, carry):
            r0 = pl.multiple_of(c * blk, blk)
            sm = small_ref[pl.ds(r0, blk), :]
            log_f = jnp.where(f_lane, -_softplus(-(sm + bf_ref[...])), 0.0)
            cs = _exact_left(tril, log_f) + carry
            parts = jnp.concatenate(_split3(cs), axis=1)
            for j in range(pairs):
                q2 = qkv_ref[pl.ds(r0, blk), j * LANES:(j + 1) * LANES].astype(F32) * scale
                k2 = qkv_ref[pl.ds(r0, blk), width + j * LANES:width + (j + 1) * LANES].astype(F32)
                v2 = qkv_ref[pl.ds(r0, blk), 2 * width + j * LANES:2 * width + (j + 1) * LANES].astype(F32)
                fq = _dot(parts, pm_ref[0, j]) + ones_ref[0:1, :]
                fk = _dot(parts, pm_ref[1, j]) + ones_ref[1:2, :]
                qaugt_ref[2 * j, c] = jnp.where(head_a, q2, fq).T.astype(BF16)
                qaugt_ref[2 * j + 1, c] = jnp.where(head_a, fq, q2).T.astype(BF16)
                kaug_ref[2 * j, c] = jnp.where(head_a, k2, fk).astype(BF16)
                kaug_ref[2 * j + 1, c] = jnp.where(head_a, fk, k2).astype(BF16)
                v_t = v2.T
                vt_ref[2 * j, c] = jnp.concatenate([v_t[:HEAD_DIM], ones_rows], axis=0).astype(BF16)
                vt_ref[2 * j + 1, c] = jnp.concatenate([v_t[HEAD_DIM:], ones_rows], axis=0).astype(BF16)
            return cs[blk - 1:blk, :]

        lax.fori_loop(0, seq // blk, body, jnp.zeros((1, LANES), F32))

    visible = _iota((blk, blk), 0) <= _iota((blk, blk), 1)

    def logits(h, kb):
        return _dot(kaug_ref[h, kb], qaugt_ref[h, qi])

    def update(h, kb, state, s_t):
        m, acc = state
        mn = jnp.maximum(m, jnp.max(s_t, axis=0, keepdims=True))
        p_t = jnp.exp(s_t - mn).astype(BF16)
        return mn, acc * jnp.exp(m - mn) + _dot(vt_ref[h, kb], p_t)

    for h in range(heads):
        s_ref[h] = logits(h, 0)

    def full_step(kb, carry):
        out = []
        for h in range(heads):
            s_t = s_ref[h]
            s_ref[h] = logits(h, kb + 1)
            out.append(update(h, kb, carry[h], s_t))
        return tuple(out)

    init = tuple((jnp.full((1, blk), -jnp.inf, F32), jnp.zeros((FOX_VT_ROWS, blk), F32)) for _ in range(heads))
    carry = lax.fori_loop(0, qi, full_step, init)
    carry = [update(h, qi, carry[h], jnp.where(visible, s_ref[h], -jnp.inf)) for h in range(heads)]

    outs = []
    for j in range(pairs):
        o_t = jnp.concatenate([carry[h][1][:HEAD_DIM] / carry[h][1][HEAD_DIM:HEAD_DIM + 1]
                               for h in (2 * j, 2 * j + 1)], axis=0)
        outs.append(o_t.T)
    o_ref[...] = _rms(jnp.concatenate(outs, axis=1), g_ref[...])


def _fox(qkv, small, bf, g, *, batch, seq, heads):
    width = heads * HEAD_DIM
    blk = FOX_BLOCK
    nblk = seq // blk
    pm, ones = _fox_bias_matrices(heads)
    const = lambda b, i: (0, 0)
    return pl.pallas_call(
        functools.partial(_fox_kernel, heads=heads),
        grid=(batch, nblk),
        in_specs=[pl.BlockSpec((seq, 3 * width), lambda b, i: (b, 0)),
                  pl.BlockSpec((seq, LANES), lambda b, i: (b, 0)),
                  pl.BlockSpec(bf.shape, const),
                  pl.BlockSpec(pm.shape, lambda b, i: (0, 0, 0, 0)),
                  pl.BlockSpec(ones.shape, const), pl.BlockSpec(g.shape, const)],
        out_specs=pl.BlockSpec((blk, width), lambda b, i: (b * nblk + i, 0)),
        out_shape=jax.ShapeDtypeStruct((batch * seq, width), F32),
        scratch_shapes=[pltpu.VMEM((heads, nblk, LANES, blk), BF16),
                        pltpu.VMEM((heads, nblk, blk, LANES), BF16),
                        pltpu.VMEM((heads, nblk, FOX_VT_ROWS, blk), BF16),
                        pltpu.VMEM((heads, blk, blk), F32)],
        compiler_params=pltpu.CompilerParams(dimension_semantics=("arbitrary", "arbitrary"),
                                             vmem_limit_bytes=VMEM_LIMIT_BYTES),
        name="fox",
    )(qkv, small, bf, pm, ones, g)


def _mixer_kernel(h_ref, ys_ref, yl_ref, yf_ref, p_ref, wo_ref, g2_ref, wg_ref, wu_ref, wd_ref,
                  g3_ref, wpg_ref, bpg_ref, wpp_ref, gf_ref, o_ref, *, ff_chunk, final_norm):
    y = jnp.concatenate([ys_ref[...], yl_ref[...], yf_ref[...]], axis=1).astype(BF16)
    h = h_ref[...] + _dot(y, wo_ref[...])
    u = _rms(h, g2_ref[...]).astype(BF16)
    d_ff = wg_ref.shape[1]
    for c0 in range(0, d_ff, ff_chunk):
        gt = _dot(u, wg_ref[:, c0:c0 + ff_chunk])
        up = _dot(u, wu_ref[:, c0:c0 + ff_chunk])
        act = (gt * _sigmoid(gt) * up).astype(BF16)
        h = h + _dot(act, wd_ref[c0:c0 + ff_chunk, :])
    u = _rms(h, g3_ref[...]).astype(BF16)
    gate = _sigmoid(_dot(u, wpg_ref[...]) + bpg_ref[...])
    h = h + gate * _dot(p_ref[...].astype(BF16), wpp_ref[...])
    if final_norm:
        h = _rms(h, gf_ref[...])
    o_ref[...] = h


def _mixer(h, ys, yl, yf, p, wo, g2, wg, wu, wd, g3, wpg, bpg, wpp, gf, *, layer, tm, ff_chunk, final_norm):
    t, d = h.shape
    const = lambda i: (0, 0)
    row = lambda i: (i, 0)
    wspec = lambda a: pl.BlockSpec(a.shape, const, pipeline_mode=pl.Buffered(1))
    vspec = lambda a: pl.BlockSpec(a.shape, const)
    return pl.pallas_call(
        functools.partial(_mixer_kernel, ff_chunk=ff_chunk, final_norm=final_norm),
        grid=(t // tm,),
        in_specs=[pl.BlockSpec((tm, d), row),
                  pl.BlockSpec((tm, ys.shape[1]), row),
                  pl.BlockSpec((tm, yl.shape[1]), row),
                  pl.BlockSpec((tm, yf.shape[1]), row),
                  pl.BlockSpec((None, tm, p.shape[2]), lambda i: (layer, i, 0)),
                  wspec(wo), vspec(g2), wspec(wg), wspec(wu), wspec(wd),
                  vspec(g3), wspec(wpg), vspec(bpg), wspec(wpp), vspec(gf)],
        out_specs=pl.BlockSpec((tm, d), row),
        out_shape=jax.ShapeDtypeStruct((t, d), F32),
        compiler_params=pltpu.CompilerParams(dimension_semantics=("arbitrary",),
                                             vmem_limit_bytes=VMEM_LIMIT_BYTES),
        name="mixer",
    )(h, ys, yl, yf, p, wo, g2, wg, wu, wd, g3, wpg, bpg, wpp, gf)


def _pad_lanes(v, col0):
    out = jnp.zeros((1, LANES), F32)
    return out.at[0, col0:col0 + v.shape[0]].set(v.astype(F32))


def _block_diag(w):
    nb, bw, _ = w.shape
    out = jnp.zeros((nb * bw, nb * bw), w.dtype)
    for g in range(nb):
        out = out.at[g * bw:(g + 1) * bw, g * bw:(g + 1) * bw].set(w[g])
    return out


def kernel(x, p, norm1_g, w_in, ssd_conv_w, ssd_conv_b, ssd_dt_bias, ssd_a_log, ssd_d, ssd_norm_g,
           lru_conv_w, lru_conv_b, lru_w_a, lru_b_a, lru_w_x, lru_b_x, lru_lambda, lru_norm_g,
           fox_b_f, fox_norm_g, w_out, norm2_g, w_gate, w_up, w_down, norm3_g, w_ple_gate,
           b_ple_gate, w_ple_proj, final_norm_g):
    batch, seq, d_model = x.shape
    depth = w_in.shape[0]
    ssd_heads = ssd_dt_bias.shape[1]
    ssd_width = ssd_heads * HEAD_DIM
    conv_dim = ssd_conv_w.shape[2]
    ssd_state = (conv_dim - ssd_width) // (2 * SSD_GROUPS)
    lru_width = lru_conv_w.shape[2]
    fox_heads = fox_b_f.shape[1]
    fox_width = fox_heads * HEAD_DIM
    tokens = batch * seq

    o_z = 0
    o_xbc = o_z + ssd_width
    o_dt = o_xbc + conv_dim
    o_lx = o_dt + ssd_heads
    o_q = o_lx + 2 * lru_width
    o_f = o_q + 3 * fox_width
    widths = (ssd_width, conv_dim, 2 * lru_width, 3 * fox_width, LANES)
    dtypes = (F32, F32, F32, BF16, F32)

    h = x.reshape(tokens, d_model)
    row = lambda v: v.reshape(1, -1).astype(F32)
    w_in_rows = w_in.reshape(depth * d_model, -1)
    p_rows = p.reshape(depth, tokens, -1)
    for i in range(depth):
        w = w_in_rows[i * d_model:(i + 1) * d_model]
        w_small = jnp.zeros((d_model, LANES), w.dtype)
        w_small = w_small.at[:, SMALL_DT_COL:SMALL_DT_COL + ssd_heads].set(w[:, o_dt:o_lx])
        w_small = w_small.at[:, SMALL_F_COL:SMALL_F_COL + fox_heads].set(w[:, o_f:o_f + fox_heads])
        w_perm = jnp.concatenate([w[:, o_z:o_dt], w[:, o_lx:o_f], w_small], axis=1).astype(BF16)

        z, xbc, lru_in, qkv, small = _in_proj(h, row(norm1_g[i]), w_perm, widths, dtypes, tm=512)

        y_ssd = _ssd(z, xbc, small, ssd_conv_w[i], row(ssd_conv_b[i]),
                     _pad_lanes(ssd_dt_bias[i], SMALL_DT_COL), _pad_lanes(ssd_a_log[i], SMALL_DT_COL),
                     row(jnp.repeat(ssd_d[i], HEAD_DIM)), row(ssd_norm_g[i]),
                     batch=batch, seq=seq, heads=ssd_heads, groups=SSD_GROUPS, state=ssd_state)
        y_lru = _lru(lru_in, lru_conv_w[i], row(lru_conv_b[i]),
                     _block_diag(lru_w_a[i]).astype(BF16), row(lru_b_a[i]),
                     _block_diag(lru_w_x[i]).astype(BF16), row(lru_b_x[i]),
                     row(lru_lambda[i]), row(lru_norm_g[i]), batch=batch, seq=seq)
        y_fox = _fox(qkv, small, _pad_lanes(fox_b_f[i], SMALL_F_COL), row(fox_norm_g[i]),
                     batch=batch, seq=seq, heads=fox_heads)
        h = _mixer(h, y_ssd, y_lru, y_fox, p_rows,
                   w_out[i].astype(BF16), row(norm2_g[i]),
                   w_gate[i].astype(BF16), w_up[i].astype(BF16), w_down[i].astype(BF16),
                   row(norm3_g[i]), w_ple_gate[i].astype(BF16), row(b_ple_gate[i]),
                   w_ple_proj[i].astype(BF16), row(final_norm_g),
                   layer=i, tm=512, ff_chunk=256, final_norm=(i == depth - 1))
    return h.reshape(batch, seq, d_model)
```

```python
import functools
import math

import jax
import jax.numpy as jnp
import numpy as np
from jax import lax
from jax.experimental import pallas as pl
from jax.experimental.pallas import tpu as pltpu

F32 = jnp.float32
BF16 = jnp.bfloat16

EPS = 1e-6
HEAD_DIM = 64
LANES = 128
BF16_SUBLANES = 16
MXU_WIDTH = 256
IN_PROJ_SUB = 256
SSD_CHUNK = 128
SSD_GROUPS = 2
LRU_C = 8.0
LOG2E = 1.4426950408889634
SMALL_DT_COL = 0
SMALL_F_COL = 8
VMEM_LIMIT_BYTES = 56 * 1024 * 1024


def _dot(a, b):
    return jnp.dot(a, b, preferred_element_type=F32)


def _dot_nt(a, b):
    return lax.dot_general(a, b, (((1,), (1,)), ((), ())), preferred_element_type=F32)


def _rms(x, g):
    ms = jnp.mean(x * x, axis=-1, keepdims=True)
    return x * lax.rsqrt(ms + EPS) * g


def _sigmoid(x):
    return 1.0 / (1.0 + jnp.exp(-x))


def _softplus(x):
    return jnp.maximum(x, 0.0) + jnp.log1p(jnp.exp(-jnp.abs(x)))


def _split3(x):
    hi = x.astype(BF16)
    r = x - hi.astype(F32)
    mid = r.astype(BF16)
    lo = (r - mid.astype(F32)).astype(BF16)
    return hi, mid, lo


def _exact_right(x, m):
    hi, mid, lo = _split3(x)
    return _dot(hi, m) + _dot(mid, m) + _dot(lo, m)


def _exact_left(m, x):
    hi, mid, lo = _split3(x)
    return _dot(m, hi) + _dot(m, mid) + _dot(m, lo)


def _iota(shape, axis):
    return lax.broadcasted_iota(jnp.int32, shape, axis)


def _causal_conv(prev, cur, w_ref, b_ref):
    ext = jnp.concatenate([prev, cur], axis=0)
    k = w_ref.shape[0]
    acc = cur * w_ref[k - 1:k, :] + b_ref[...]
    for j in range(1, k):
        acc = acc + pltpu.roll(ext, j, 0)[8:, :] * w_ref[k - 1 - j:k - j, :]
    return acc


def _silu(x):
    return x * _sigmoid(x)


def _gelu_tanh(x):
    return 0.5 * x * (1.0 + jnp.tanh(0.7978845608028654 * (x + 0.044715 * (x * x * x))))


def _in_proj_kernel(h_ref, g_ref, w_ref, *out_refs, outs, groups, sub):
    nsub = h_ref.shape[0] // sub

    def products(s):
        u = _rms(h_ref[s * sub:(s + 1) * sub, :], g_ref[...]).astype(BF16)
        return [_dot(u, w_ref[:, a:b]) for a, b in groups]

    def finish(s, prods):
        for o_ref, (gi, a, b, gain, tblk, epi) in zip(out_refs, outs):
            y = prods[gi][:, a:b]
            if gain != 1.0:
                y = y * gain
            if epi == "silu":
                y = _silu(y)
            elif epi == "gelu":
                y = _gelu_tanh(y)
            if tblk is None:
                o_ref[s * sub:(s + 1) * sub, :] = y.astype(o_ref.dtype)
            else:
                for t in range(sub // tblk):
                    o_ref[s * (sub // tblk) + t] = y[t * tblk:(t + 1) * tblk, :].T.astype(o_ref.dtype)

    pending = products(0)
    for s in range(1, nsub):
        nxt = products(s)
        finish(s - 1, pending)
        pending = nxt
    finish(nsub - 1, pending)


def _in_proj(h, g, w, outs, tm):
    t, d = h.shape
    specs, shapes, cfg, groups, off, g0 = [], [], [], [], 0, 0
    for wd, dt, gain, tblk, epi in outs:
        cfg.append((len(groups), off - g0, off - g0 + wd, gain, tblk, epi))
        off += wd
        if (off - g0) % MXU_WIDTH == 0:
            groups.append((g0, off))
            g0 = off
        if tblk is None:
            specs.append(pl.BlockSpec((tm, wd), lambda i: (i, 0)))
            shapes.append(jax.ShapeDtypeStruct((t, wd), dt))
        else:
            specs.append(pl.BlockSpec((tm // tblk, wd, tblk), lambda i: (i, 0, 0)))
            shapes.append(jax.ShapeDtypeStruct((t // tblk, wd, tblk), dt))
    assert g0 == off == w.shape[1], "output widths must tile the weight columns in MXU-width groups"
    const = lambda i: (0, 0)
    return pl.pallas_call(
        functools.partial(_in_proj_kernel, outs=tuple(cfg), groups=tuple(groups), sub=IN_PROJ_SUB),
        grid=(t // tm,),
        in_specs=[pl.BlockSpec((tm, d), lambda i: (i, 0)),
                  pl.BlockSpec((1, d), const),
                  pl.BlockSpec(w.shape, const, pipeline_mode=pl.Buffered(1))],
        out_specs=specs,
        out_shape=shapes,
        compiler_params=pltpu.CompilerParams(dimension_semantics=("arbitrary",),
                                             vmem_limit_bytes=VMEM_LIMIT_BYTES),
        name="in_proj",
    )(h, g, w)


def _ssd_kernel(zs_ref, xbc_ref, small_ref, cw_ref, cb_ref, dtb_ref, alog_ref, dsk_ref, g_ref,
                o_ref, st_ref, *, heads, groups, state):
    seq = xbc_ref.shape[0]
    L = SSD_CHUNK
    width = heads * HEAD_DIM
    pairs = width // LANES
    hpg = heads // groups
    causal = _iota((L, L), 0) >= _iota((L, L), 1)
    tril = causal.astype(BF16)
    expand = (_iota((LANES, width), 0) + SMALL_DT_COL == _iota((LANES, width), 1) // HEAD_DIM).astype(BF16)
    head_lane = (_iota((1, LANES), 1) >= SMALL_DT_COL) & (_iota((1, LANES), 1) < SMALL_DT_COL + heads)
    first_half = _iota((L, LANES), 1) < HEAD_DIM
    bd_rows = _iota((2 * state, LANES), 0) < state
    bd_mask = bd_rows == (_iota((2 * state, LANES), 1) < HEAD_DIM)
    a_row = jnp.where(head_lane, -jnp.exp(alog_ref[...]), 0.0)

    st_ref[...] = jnp.zeros_like(st_ref)

    def chunk(c):
        r0 = c * L
        prev = xbc_ref[r0 - 8:r0, :] if c > 0 else jnp.zeros((8, xbc_ref.shape[1]), F32)
        xc = _silu(_causal_conv(prev, xbc_ref[r0:r0 + L, :], cw_ref, cb_ref))
        xs = xc[:, :width]
        bm = xc[:, width:width + groups * state]
        cm = xc[:, width + groups * state:]
        sm = small_ref[pl.ds(r0, L), :]
        dt = jnp.where(head_lane, _softplus(sm + dtb_ref[...]), 0.0)
        acs = _exact_left(tril, dt * a_row) * LOG2E
        acs_t = acs.T
        dt_e = _exact_right(dt, expand)
        acs_e = _exact_right(acs, expand)
        last_e = acs_e[L - 1:L, :]
        xdt = xs * dt_e
        xdw = xdt * jnp.exp2(last_e - acs_e)
        eacs = jnp.exp2(acs_e)
        cdec = jnp.exp2(last_e)
        bm_b = bm.astype(BF16)
        cm_b = cm.astype(BF16)
        cb = [_dot_nt(cm_b[:, g * state:(g + 1) * state], bm_b[:, g * state:(g + 1) * state])
              for g in range(groups)]
        sc = []
        for h in range(heads):
            seg = acs[:, h:h + 1] - acs_t[h:h + 1, :]
            lm = jnp.exp2(jnp.where(causal, seg, -jnp.inf))
            sc.append((cb[h // hpg] * lm).astype(BF16))
        ys = []
        for j in range(pairs):
            ha, hb = 2 * j, 2 * j + 1
            ga, gb = ha // hpg, hb // hpg
            cs = slice(j * LANES, (j + 1) * LANES)
            xp = xdt[:, cs]
            x_bd = jnp.concatenate([jnp.where(first_half, xp, 0.0), jnp.where(first_half, 0.0, xp)],
                                   axis=0).astype(BF16)
            yd = _dot(jnp.concatenate([sc[ha], sc[hb]], axis=1), x_bd)
            cm_p = jnp.concatenate([cm_b[:, ga * state:(ga + 1) * state],
                                    cm_b[:, gb * state:(gb + 1) * state]], axis=1)
            st = st_ref[j]
            yo = _dot(cm_p, st.astype(BF16)) * eacs[:, cs]
            bm_pt = jnp.concatenate([bm[:, ga * state:(ga + 1) * state],
                                     bm[:, gb * state:(gb + 1) * state]], axis=1).T.astype(BF16)
            new = _dot(bm_pt, xdw[:, cs].astype(BF16))
            st_ref[j] = st * cdec[:, cs] + jnp.where(bd_mask, new, 0.0)
            ys.append(yd + yo + xs[:, cs] * dsk_ref[:, cs])
        y = jnp.concatenate(ys, axis=1)
        o_ref[pl.ds(r0, L), :] = _rms(y * zs_ref[pl.ds(r0, L), :], g_ref[...])

    for c in range(seq // L):
        chunk(c)


def _ssd(zs, xbc, small, cw, cb, dtb, alog, dsk, g, *, batch, seq, heads, groups, state):
    width = heads * HEAD_DIM
    const = lambda b: (0, 0)
    row = lambda b: (b, 0)
    return pl.pallas_call(
        functools.partial(_ssd_kernel, heads=heads, groups=groups, state=state),
        grid=(batch,),
        in_specs=[pl.BlockSpec((seq, width), row),
                  pl.BlockSpec((seq, xbc.shape[1]), row),
                  pl.BlockSpec((seq, LANES), row),
                  pl.BlockSpec(cw.shape, const), pl.BlockSpec(cb.shape, const),
                  pl.BlockSpec(dtb.shape, const), pl.BlockSpec(alog.shape, const),
                  pl.BlockSpec(dsk.shape, const), pl.BlockSpec(g.shape, const)],
        out_specs=pl.BlockSpec((seq, width), row),
        out_shape=jax.ShapeDtypeStruct((batch * seq, width), F32),
        scratch_shapes=[pltpu.VMEM((width // LANES, 2 * state, LANES), F32)],
        compiler_params=pltpu.CompilerParams(dimension_semantics=("arbitrary",),
                                             vmem_limit_bytes=VMEM_LIMIT_BYTES),
        name="ssd",
    )(zs, xbc, small, cw, cb, dtb, alog, dsk, g)


def _lru_kernel(x_ref, gg_ref, cw_ref, cb_ref, wa_ref, ba_ref, wx_ref, bx_ref, lam_ref, g_ref, o_ref,
                *, chunk):
    seq = x_ref.shape[0]
    w = o_ref.shape[1]
    sp = _softplus(-lam_ref[...])
    rowi = _iota((chunk, w), 0)

    def body(c, hprev):
        r0 = c * chunk
        prev = x_ref[r0 - 8:r0, :] if c > 0 else jnp.zeros((8, w), F32)
        xl = _causal_conv(prev, x_ref[r0:r0 + chunk, :], cw_ref, cb_ref)
        xb = xl.astype(BF16)
        r = _sigmoid(_dot(xb, wa_ref[...]) + ba_ref[...])
        i = _sigmoid(_dot(xb, wx_ref[...]) + bx_ref[...])
        log_a = (-LRU_C) * r * sp
        a = jnp.exp(log_a)
        u = jnp.sqrt(-jnp.tanh(log_a) * (a * a + 1.0)) * (i * xl)
        d = 1
        while d < chunk:
            if d < 8:
                keep = rowi >= d
                a_sh = jnp.where(keep, pltpu.roll(a, d, 0), 1.0)
                u_sh = jnp.where(keep, pltpu.roll(u, d, 0), 0.0)
            else:
                a_sh = jnp.concatenate([jnp.ones((d, w), F32), a[:chunk - d]], axis=0)
                u_sh = jnp.concatenate([jnp.zeros((d, w), F32), u[:chunk - d]], axis=0)
            u = a * u_sh + u
            a = a * a_sh
            d *= 2
        h = a * hprev + u
        o_ref[pl.ds(r0, chunk), :] = _rms(h * gg_ref[pl.ds(r0, chunk), :], g_ref[...])
        return h[chunk - 1:chunk, :]

    hprev = jnp.zeros((1, w), F32)
    for c in range(seq // chunk):
        hprev = body(c, hprev)


def _lru(x, gg, cw, cb, wa, ba, wx, bx, lam, g, *, batch, seq, chunk=256):
    w = x.shape[1]
    const = lambda b: (0, 0)
    row = lambda b: (b, 0)
    return pl.pallas_call(
        functools.partial(_lru_kernel, chunk=chunk),
        grid=(batch,),
        in_specs=[pl.BlockSpec((seq, w), row), pl.BlockSpec((seq, w), row),
                  pl.BlockSpec(cw.shape, const), pl.BlockSpec(cb.shape, const),
                  pl.BlockSpec(wa.shape, const), pl.BlockSpec(ba.shape, const),
                  pl.BlockSpec(wx.shape, const), pl.BlockSpec(bx.shape, const),
                  pl.BlockSpec(lam.shape, const), pl.BlockSpec(g.shape, const)],
        out_specs=pl.BlockSpec((seq, w), row),
        out_shape=jax.ShapeDtypeStruct((batch * seq, w), F32),
        compiler_params=pltpu.CompilerParams(dimension_semantics=("arbitrary",),
                                             vmem_limit_bytes=VMEM_LIMIT_BYTES),
        name="rglru",
    )(x, gg, cw, cb, wa, ba, wx, bx, lam, g)


FOX_BLOCK = 256
FOX_PART_STRIDE = 16
FOX_VT_ROWS = HEAD_DIM + BF16_SUBLANES


def _fox_bias_matrices(heads):
    pairs = heads * HEAD_DIM // LANES
    assert SMALL_F_COL + heads <= FOX_PART_STRIDE
    pm_q = np.zeros((pairs * LANES, LANES), np.float32)
    pm_k = np.zeros((LANES, pairs * LANES), np.float32)
    ones_q = np.zeros((LANES, 1), np.float32)
    ones_k = np.zeros((1, LANES), np.float32)
    for j in range(pairs):
        for part in range(3):
            ca = part * FOX_PART_STRIDE + SMALL_F_COL + 2 * j
            pm_q[j * LANES + HEAD_DIM + part, ca] = 1.0
            pm_q[j * LANES + part, ca + 1] = 1.0
            pm_k[ca, j * LANES + HEAD_DIM + 3 + part] = -1.0
            pm_k[ca + 1, j * LANES + 3 + part] = -1.0
    for base in (0, HEAD_DIM):
        ones_q[base + 3:base + 6, 0] = 1.0
        ones_k[0, base:base + 3] = 1.0
    return jnp.asarray(pm_q, BF16), jnp.asarray(pm_k, BF16), jnp.asarray(ones_q, F32), jnp.asarray(ones_k, F32)


def _fox_kernel(qt_ref, k_ref, vt_in_ref, small_ref, bf_ref, pmq_ref, pmk_ref, onesq_ref, onesk_ref, g_ref,
                o_ref, qaugt_ref, kaug_ref, vt_ref, s_ref, *, heads):
    seq = k_ref.shape[0]
    width = heads * HEAD_DIM
    pairs = width // LANES
    blk = FOX_BLOCK
    qi = pl.program_id(1)

    @pl.when(qi == 0)
    def _():
        tril = (_iota((blk, blk), 0) >= _iota((blk, blk), 1)).astype(BF16)
        f_lane = (_iota((1, LANES), 1) >= SMALL_F_COL) & (_iota((1, LANES), 1) < SMALL_F_COL + heads)
        head_a = _iota((blk, LANES), 1) < HEAD_DIM
        head_a_t = _iota((LANES, blk), 0) < HEAD_DIM
        ones_rows = jnp.where(_iota((BF16_SUBLANES, blk), 0) == 0, 1.0, 0.0).astype(BF16)

        def body(c, carry):
            r0 = pl.multiple_of(c * blk, blk)
            sm = small_ref[pl.ds(r0, blk), :]
            log_f = jnp.where(f_lane, -_softplus(-(sm + bf_ref[...])), 0.0)
            cs = _exact_left(tril, log_f) + carry
            hi, mid, lo = _split3(cs * LOG2E)
            packed = (hi.astype(F32) + pltpu.roll(mid.astype(F32), FOX_PART_STRIDE, 1)
                      + pltpu.roll(lo.astype(F32), 2 * FOX_PART_STRIDE, 1)).astype(BF16)
            fq_all = _dot_nt(pmq_ref[...], packed)
            fk_all = _dot(packed, pmk_ref[...])
            q_t = qt_ref[c]
            v_t = vt_in_ref[c]
            for j in range(pairs):
                rows = slice(j * LANES, (j + 1) * LANES)
                fq_t = (fq_all[rows] + onesq_ref[...]).astype(BF16)
                fk = (fk_all[:, rows] + onesk_ref[...]).astype(BF16)
                k2 = k_ref[pl.ds(r0, blk), rows]
                qaugt_ref[2 * j, c] = jnp.where(head_a_t, q_t[rows], fq_t)
                qaugt_ref[2 * j + 1, c] = jnp.where(head_a_t, fq_t, q_t[rows])
                kaug_ref[2 * j, c] = jnp.where(head_a, k2, fk)
                kaug_ref[2 * j + 1, c] = jnp.where(head_a, fk, k2)
                vt_ref[2 * j, c] = jnp.concatenate([v_t[j * LANES:j * LANES + HEAD_DIM], ones_rows], axis=0)
                vt_ref[2 * j + 1, c] = jnp.concatenate([v_t[j * LANES + HEAD_DIM:(j + 1) * LANES], ones_rows],
                                                       axis=0)
            return cs[blk - 1:blk, :]

        carry = jnp.zeros((1, LANES), F32)
        for c in range(seq // blk):
            carry = body(c, carry)

    visible = _iota((blk, blk), 0) <= _iota((blk, blk), 1)

    def logits(h, kb):
        return _dot(kaug_ref[h, kb], qaugt_ref[h, qi])

    def update(h, kb, state, s_t):
        m, acc = state
        mn = jnp.maximum(m, jnp.max(s_t, axis=0, keepdims=True))
        p_t = jnp.exp2(s_t - mn).astype(BF16)
        return mn, acc * jnp.exp2(m - mn) + _dot(vt_ref[h, kb], p_t)

    for h in range(heads):
        s_ref[h] = logits(h, 0)

    def full_step(kb, carry):
        out = []
        for h in range(heads):
            s_t = s_ref[h]
            s_ref[h] = logits(h, kb + 1)
            out.append(update(h, kb, carry[h], s_t))
        return tuple(out)

    def double_step(t, carry):
        return full_step(2 * t + 1, full_step(2 * t, carry))

    init = tuple((jnp.full((1, blk), -jnp.inf, F32), jnp.zeros((FOX_VT_ROWS, blk), F32)) for _ in range(heads))
    carry = lax.fori_loop(0, qi // 2, double_step, init)
    carry = lax.fori_loop(2 * (qi // 2), qi, full_step, carry)
    carry = [update(h, qi, carry[h], jnp.where(visible, s_ref[h], -jnp.inf)) for h in range(heads)]

    outs = []
    for j in range(pairs):
        o_t = jnp.concatenate([carry[h][1][:HEAD_DIM] / carry[h][1][HEAD_DIM:HEAD_DIM + 1]
                               for h in (2 * j, 2 * j + 1)], axis=0)
        outs.append(o_t.T)
    o_ref[...] = _rms(jnp.concatenate(outs, axis=1), g_ref[...])


def _fox(q_t, k, v_t, small, bf, g, *, batch, seq, heads):
    width = heads * HEAD_DIM
    blk = FOX_BLOCK
    nblk = seq // blk
    pm_q, pm_k, ones_q, ones_k = _fox_bias_matrices(heads)
    const = lambda b, i: (0, 0)
    const3 = lambda b, i: (0, 0, 0)
    return pl.pallas_call(
        functools.partial(_fox_kernel, heads=heads),
        grid=(batch, nblk),
        in_specs=[pl.BlockSpec((nblk, width, blk), lambda b, i: (b, 0, 0)),
                  pl.BlockSpec((seq, width), lambda b, i: (b, 0)),
                  pl.BlockSpec((nblk, width, blk), lambda b, i: (b, 0, 0)),
                  pl.BlockSpec((seq, LANES), lambda b, i: (b, 0)),
                  pl.BlockSpec(bf.shape, const),
                  pl.BlockSpec(pm_q.shape, const), pl.BlockSpec(pm_k.shape, const),
                  pl.BlockSpec(ones_q.shape, const), pl.BlockSpec(ones_k.shape, const),
                  pl.BlockSpec(g.shape, const)],
        out_specs=pl.BlockSpec((blk, width), lambda b, i: (b * nblk + i, 0)),
        out_shape=jax.ShapeDtypeStruct((batch * seq, width), F32),
        scratch_shapes=[pltpu.VMEM((heads, nblk, LANES, blk), BF16),
                        pltpu.VMEM((heads, nblk, blk, LANES), BF16),
                        pltpu.VMEM((heads, nblk, FOX_VT_ROWS, blk), BF16),
                        pltpu.VMEM((heads, blk, blk), F32)],
        compiler_params=pltpu.CompilerParams(dimension_semantics=("arbitrary", "arbitrary"),
                                             vmem_limit_bytes=VMEM_LIMIT_BYTES),
        name="fox",
    )(q_t, k, v_t, small, bf, pm_q, pm_k, ones_q, ones_k, g)


def _mixer_kernel(h_ref, ys_ref, yl_ref, yf_ref, p_ref, wo_ref, g2_ref, wg_ref, wu_ref, wd_ref,
                  g3_ref, wpg_ref, bpg_ref, wpp_ref, gf_ref, o_ref, *, ff_chunk, final_norm):
    y = jnp.concatenate([ys_ref[...], yl_ref[...], yf_ref[...]], axis=1).astype(BF16)
    h = h_ref[...] + _dot(y, wo_ref[...])
    u = _rms(h, g2_ref[...]).astype(BF16)
    d_ff = wg_ref.shape[1]
    for c0 in range(0, d_ff, ff_chunk):
        gt = _dot(u, wg_ref[:, c0:c0 + ff_chunk])
        up = _dot(u, wu_ref[:, c0:c0 + ff_chunk])
        act = (gt * _sigmoid(gt) * up).astype(BF16)
        h = h + _dot(act, wd_ref[c0:c0 + ff_chunk, :])
    u = _rms(h, g3_ref[...]).astype(BF16)
    gate = _sigmoid(_dot(u, wpg_ref[...]) + bpg_ref[...])
    h = h + gate * _dot(p_ref[...].astype(BF16), wpp_ref[...])
    if final_norm:
        h = _rms(h, gf_ref[...])
    o_ref[...] = h


def _mixer(h, ys, yl, yf, p, wo, g2, wg, wu, wd, g3, wpg, bpg, wpp, gf, *, layer, tm, ff_chunk, final_norm):
    t, d = h.shape
    const = lambda i: (0, 0)
    row = lambda i: (i, 0)
    wspec = lambda a: pl.BlockSpec(a.shape, const, pipeline_mode=pl.Buffered(1))
    vspec = lambda a: pl.BlockSpec(a.shape, const)
    return pl.pallas_call(
        functools.partial(_mixer_kernel, ff_chunk=ff_chunk, final_norm=final_norm),
        grid=(t // tm,),
        in_specs=[pl.BlockSpec((tm, d), row),
                  pl.BlockSpec((tm, ys.shape[1]), row),
                  pl.BlockSpec((tm, yl.shape[1]), row),
                  pl.BlockSpec((tm, yf.shape[1]), row),
                  pl.BlockSpec((None, tm, p.shape[2]), lambda i: (layer, i, 0)),
                  wspec(wo), vspec(g2), wspec(wg), wspec(wu), wspec(wd),
                  vspec(g3), wspec(wpg), vspec(bpg), wspec(wpp), vspec(gf)],
        out_specs=pl.BlockSpec((tm, d), row),
        out_shape=jax.ShapeDtypeStruct((t, d), F32),
        compiler_params=pltpu.CompilerParams(dimension_semantics=("arbitrary",),
                                             vmem_limit_bytes=VMEM_LIMIT_BYTES),
        name="mixer",
    )(h, ys, yl, yf, p, wo, g2, wg, wu, wd, g3, wpg, bpg, wpp, gf)


def _pad_lanes(v, col0):
    out = jnp.zeros((1, LANES), F32)
    return out.at[0, col0:col0 + v.shape[0]].set(v.astype(F32))


def _block_diag(w):
    nb, bw, _ = w.shape
    out = jnp.zeros((nb * bw, nb * bw), w.dtype)
    for g in range(nb):
        out = out.at[g * bw:(g + 1) * bw, g * bw:(g + 1) * bw].set(w[g])
    return out


def kernel(x, p, norm1_g, w_in, ssd_conv_w, ssd_conv_b, ssd_dt_bias, ssd_a_log, ssd_d, ssd_norm_g,
           lru_conv_w, lru_conv_b, lru_w_a, lru_b_a, lru_w_x, lru_b_x, lru_lambda, lru_norm_g,
           fox_b_f, fox_norm_g, w_out, norm2_g, w_gate, w_up, w_down, norm3_g, w_ple_gate,
           b_ple_gate, w_ple_proj, final_norm_g):
    batch, seq, d_model = x.shape
    depth = w_in.shape[0]
    ssd_heads = ssd_dt_bias.shape[1]
    ssd_width = ssd_heads * HEAD_DIM
    conv_dim = ssd_conv_w.shape[2]
    ssd_state = (conv_dim - ssd_width) // (2 * SSD_GROUPS)
    lru_width = lru_conv_w.shape[2]
    fox_heads = fox_b_f.shape[1]
    fox_width = fox_heads * HEAD_DIM
    tokens = batch * seq

    o_z = 0
    o_xbc = o_z + ssd_width
    o_dt = o_xbc + conv_dim
    o_lx = o_dt + ssd_heads
    o_q = o_lx + 2 * lru_width
    o_f = o_q + 3 * fox_width
    q_gain = HEAD_DIM ** -0.5 * LOG2E
    outs = ((ssd_width, F32, 1.0, None, "silu"), (conv_dim, F32, 1.0, None, None),
            (lru_width, F32, 1.0, None, None), (lru_width, F32, 1.0, None, "gelu"),
            (fox_width, BF16, q_gain, FOX_BLOCK, None), (fox_width, BF16, 1.0, None, None),
            (fox_width, BF16, 1.0, FOX_BLOCK, None), (LANES, F32, 1.0, None, None))

    h = x.reshape(tokens, d_model)
    row = lambda v: v.reshape(1, -1).astype(F32)
    w_in_rows = w_in.reshape(depth * d_model, -1)
    p_rows = p.reshape(depth, tokens, -1)
    for i in range(depth):
        w = w_in_rows[i * d_model:(i + 1) * d_model]
        w_small = jnp.zeros((d_model, LANES), w.dtype)
        w_small = w_small.at[:, SMALL_DT_COL:SMALL_DT_COL + ssd_heads].set(w[:, o_dt:o_lx])
        w_small = w_small.at[:, SMALL_F_COL:SMALL_F_COL + fox_heads].set(w[:, o_f:o_f + fox_heads])
        w_perm = jnp.concatenate([w[:, o_z:o_dt], w[:, o_lx:o_f], w_small], axis=1).astype(BF16)

        zs, xbc, lru_x, lru_gate, q_t, k, v_t, small = _in_proj(h, row(norm1_g[i]), w_perm, outs, tm=1024)

        y_ssd = _ssd(zs, xbc, small, ssd_conv_w[i], row(ssd_conv_b[i]),
                     _pad_lanes(ssd_dt_bias[i], SMALL_DT_COL), _pad_lanes(ssd_a_log[i], SMALL_DT_COL),
                     row(jnp.repeat(ssd_d[i], HEAD_DIM)), row(ssd_norm_g[i]),
                     batch=batch, seq=seq, heads=ssd_heads, groups=SSD_GROUPS, state=ssd_state)
        y_lru = _lru(lru_x, lru_gate, lru_conv_w[i], row(lru_conv_b[i]),
                     _block_diag(lru_w_a[i]).astype(BF16), row(lru_b_a[i]),
                     _block_diag(lru_w_x[i]).astype(BF16), row(lru_b_x[i]),
                     row(lru_lambda[i]), row(lru_norm_g[i]), batch=batch, seq=seq)
        y_fox = _fox(q_t, k, v_t, small, _pad_lanes(fox_b_f[i], SMALL_F_COL), row(fox_norm_g[i]),
                     batch=batch, seq=seq, heads=fox_heads)
        h = _mixer(h, y_ssd, y_lru, y_fox, p_rows,
                   w_out[i].astype(BF16), row(norm2_g[i]),
                   w_gate[i].astype(BF16), w_up[i].astype(BF16), w_down[i].astype(BF16),
                   row(norm3_g[i]), w_ple_gate[i].astype(BF16), row(b_ple_gate[i]),
                   w_ple_proj[i].astype(BF16), row(final_norm_g),
                   layer=i, tm=512, ff_chunk=256, final_norm=(i == depth - 1))
    return h.reshape(batch, seq, d_model)
```

```python
import functools
import math

import jax
import jax.numpy as jnp
import numpy as np
from jax import lax
from jax.experimental import pallas as pl
from jax.experimental.pallas import tpu as pltpu

F32 = jnp.float32
BF16 = jnp.bfloat16

EPS = 1e-6
HEAD_DIM = 64
LANES = 128
BF16_SUBLANES = 16
MXU_WIDTH = 256
IN_PROJ_SUB = 256
SSD_CHUNK = 128
SSD_GROUPS = 2
LRU_C = 8.0
LOG2E = 1.4426950408889634
SMALL_DT_COL = 0
SMALL_F_COL = 8
VMEM_LIMIT_BYTES = 56 * 1024 * 1024


def _dot(a, b):
    return jnp.dot(a, b, preferred_element_type=F32)


def _dot_nt(a, b):
    return lax.dot_general(a, b, (((1,), (1,)), ((), ())), preferred_element_type=F32)


def _rms(x, g):
    ms = jnp.mean(x * x, axis=-1, keepdims=True)
    return x * lax.rsqrt(ms + EPS) * g


def _sigmoid(x):
    return 1.0 / (1.0 + jnp.exp(-x))


def _softplus(x):
    return jnp.maximum(x, 0.0) + jnp.log1p(jnp.exp(-jnp.abs(x)))


def _split3(x):
    hi = x.astype(BF16)
    r = x - hi.astype(F32)
    mid = r.astype(BF16)
    lo = (r - mid.astype(F32)).astype(BF16)
    return hi, mid, lo


def _exact_right(x, m):
    hi, mid, lo = _split3(x)
    return _dot(hi, m) + _dot(mid, m) + _dot(lo, m)


def _exact_left(m, x):
    hi, mid, lo = _split3(x)
    return _dot(m, hi) + _dot(m, mid) + _dot(m, lo)


def _iota(shape, axis):
    return lax.broadcasted_iota(jnp.int32, shape, axis)


def _causal_conv(prev, cur, w_ref, b_ref):
    ext = jnp.concatenate([prev, cur], axis=0)
    k = w_ref.shape[0]
    acc = cur * w_ref[k - 1:k, :] + b_ref[...]
    for j in range(1, k):
        acc = acc + pltpu.roll(ext, j, 0)[8:, :] * w_ref[k - 1 - j:k - j, :]
    return acc


def _silu(x):
    return x * _sigmoid(x)


def _gelu_tanh(x):
    return 0.5 * x * (1.0 + jnp.tanh(0.7978845608028654 * (x + 0.044715 * (x * x * x))))


def _in_proj_kernel(h_ref, g_ref, w_ref, *out_refs, outs, groups, sub):
    nsub = h_ref.shape[0] // sub

    def products(s):
        u = _rms(h_ref[s * sub:(s + 1) * sub, :], g_ref[...]).astype(BF16)
        return [_dot(u, w_ref[:, a:b]) for a, b in groups]

    def finish(s, prods):
        for o_ref, (gi, a, b, gain, tblk, epi) in zip(out_refs, outs):
            y = prods[gi][:, a:b]
            if gain != 1.0:
                y = y * gain
            if epi == "silu":
                y = _silu(y)
            elif epi == "gelu":
                y = _gelu_tanh(y)
            if tblk is None:
                o_ref[s * sub:(s + 1) * sub, :] = y.astype(o_ref.dtype)
            else:
                for t in range(sub // tblk):
                    o_ref[s * (sub // tblk) + t] = y[t * tblk:(t + 1) * tblk, :].T.astype(o_ref.dtype)

    pending = products(0)
    for s in range(1, nsub):
        nxt = products(s)
        finish(s - 1, pending)
        pending = nxt
    finish(nsub - 1, pending)


def _in_proj(h, g, w, outs, tm):
    t, d = h.shape
    specs, shapes, cfg, groups, off, g0 = [], [], [], [], 0, 0
    for wd, dt, gain, tblk, epi in outs:
        cfg.append((len(groups), off - g0, off - g0 + wd, gain, tblk, epi))
        off += wd
        if (off - g0) % MXU_WIDTH == 0:
            groups.append((g0, off))
            g0 = off
        if tblk is None:
            specs.append(pl.BlockSpec((tm, wd), lambda i: (i, 0)))
            shapes.append(jax.ShapeDtypeStruct((t, wd), dt))
        else:
            specs.append(pl.BlockSpec((tm // tblk, wd, tblk), lambda i: (i, 0, 0)))
            shapes.append(jax.ShapeDtypeStruct((t // tblk, wd, tblk), dt))
    assert g0 == off == w.shape[1], "output widths must tile the weight columns in MXU-width groups"
    const = lambda i: (0, 0)
    return pl.pallas_call(
        functools.partial(_in_proj_kernel, outs=tuple(cfg), groups=tuple(groups), sub=IN_PROJ_SUB),
        grid=(t // tm,),
        in_specs=[pl.BlockSpec((tm, d), lambda i: (i, 0)),
                  pl.BlockSpec((1, d), const),
                  pl.BlockSpec(w.shape, const, pipeline_mode=pl.Buffered(1))],
        out_specs=specs,
        out_shape=shapes,
        compiler_params=pltpu.CompilerParams(dimension_semantics=("arbitrary",),
                                             vmem_limit_bytes=VMEM_LIMIT_BYTES),
        name="in_proj",
    )(h, g, w)


def _ssd_kernel(zs_ref, xbc_ref, small_ref, cw_ref, cb_ref, dtb_ref, alog_ref, dsk_ref, g_ref,
                o_ref, *, heads, groups, state):
    seq = xbc_ref.shape[0]
    L = SSD_CHUNK
    width = heads * HEAD_DIM
    pairs = width // LANES
    hpg = heads // groups
    causal = _iota((L, L), 0) >= _iota((L, L), 1)
    tril = causal.astype(BF16)
    head_lane = (_iota((1, LANES), 1) >= SMALL_DT_COL) & (_iota((1, LANES), 1) < SMALL_DT_COL + heads)
    first_half = _iota((L, LANES), 1) < HEAD_DIM
    a_row = jnp.where(head_lane, -jnp.exp(alog_ref[...]), 0.0)
    win_tiles = MXU_WIDTH // LANES
    win0 = [g * hpg * HEAD_DIM // LANES for g in range(groups)]
    assert all((g + 1) * hpg * HEAD_DIM <= (win0[g] + win_tiles) * LANES <= width for g in range(groups))

    def chunk(c, states):
        r0 = c * L
        prev = xbc_ref[r0 - 8:r0, :] if c > 0 else jnp.zeros((8, xbc_ref.shape[1]), F32)
        xc = _silu(_causal_conv(prev, xbc_ref[r0:r0 + L, :], cw_ref, cb_ref))
        bm = xc[:, width:width + groups * state]
        cm_b = xc[:, width + groups * state:].astype(BF16)
        bm_b = bm.astype(BF16)
        sm = small_ref[pl.ds(r0, L), :]
        dt = jnp.where(head_lane, _softplus(sm + dtb_ref[...]), 0.0)
        acs = _exact_left(tril, dt * a_row) * LOG2E
        acs_t = acs.T
        acs_b = [jnp.broadcast_to(acs[:, SMALL_DT_COL + h:SMALL_DT_COL + h + 1], (L, LANES)) for h in range(heads)]
        dt_b = [jnp.broadcast_to(dt[:, SMALL_DT_COL + h:SMALL_DT_COL + h + 1], (L, LANES)) for h in range(heads)]
        cb = [_dot_nt(cm_b[:, g * state:(g + 1) * state], bm_b[:, g * state:(g + 1) * state])
              for g in range(groups)]
        sc = []
        for h in range(heads):
            seg = acs_b[h] - acs_t[SMALL_DT_COL + h:SMALL_DT_COL + h + 1, :]
            lm = jnp.exp2(jnp.where(causal, seg, -jnp.inf))
            sc.append((cb[h // hpg] * lm).astype(BF16))
        xs, xdt, xdw, eacs, cdec, yd = [], [], [], [], [], []
        for j in range(pairs):
            ha, hb = 2 * j, 2 * j + 1
            dt_e = jnp.where(first_half, dt_b[ha], dt_b[hb])
            acs_e = jnp.where(first_half, acs_b[ha], acs_b[hb])
            last_e = acs_e[L - 1:L, :]
            xs.append(xc[:, j * LANES:(j + 1) * LANES])
            xdt.append(xs[j] * dt_e)
            xdw.append((xdt[j] * jnp.exp2(last_e - acs_e)).astype(BF16))
            eacs.append(jnp.exp2(acs_e))
            cdec.append(jnp.exp2(last_e))
            x_bd = jnp.concatenate([jnp.where(first_half, xdt[j], 0.0), jnp.where(first_half, 0.0, xdt[j])],
                                   axis=0).astype(BF16)
            yd.append(_dot(jnp.concatenate([sc[ha], sc[hb]], axis=1), x_bd))
        yo, new_states = [], []
        for g in range(groups):
            tiles = range(win0[g], win0[g] + win_tiles)
            yo.append(_dot(cm_b[:, g * state:(g + 1) * state], states[g].astype(BF16)))
            bm_t = bm[:, g * state:(g + 1) * state].T.astype(BF16)
            new = _dot(bm_t, jnp.concatenate([xdw[t] for t in tiles], axis=1))
            new_states.append(states[g] * jnp.concatenate([cdec[t] for t in tiles], axis=1) + new)
        ys = []
        for j in range(pairs):
            ga, gb = 2 * j // hpg, (2 * j + 1) // hpg
            yo_a = yo[ga][:, (j - win0[ga]) * LANES:(j - win0[ga] + 1) * LANES]
            yo_b = yo[gb][:, (j - win0[gb]) * LANES:(j - win0[gb] + 1) * LANES]
            yo_j = yo_a if ga == gb else jnp.where(first_half, yo_a, yo_b)
            ys.append(yd[j] + yo_j * eacs[j] + xs[j] * dsk_ref[:, j * LANES:(j + 1) * LANES])
        y = jnp.concatenate(ys, axis=1)
        o_ref[pl.ds(r0, L), :] = _rms(y * zs_ref[pl.ds(r0, L), :], g_ref[...])
        return new_states

    states = [jnp.zeros((state, MXU_WIDTH), F32) for _ in range(groups)]
    for c in range(seq // L):
        states = chunk(c, states)


def _ssd(zs, xbc, small, cw, cb, dtb, alog, dsk, g, *, batch, seq, heads, groups, state):
    width = heads * HEAD_DIM
    const = lambda b: (0, 0)
    row = lambda b: (b, 0)
    return pl.pallas_call(
        functools.partial(_ssd_kernel, heads=heads, groups=groups, state=state),
        grid=(batch,),
        in_specs=[pl.BlockSpec((seq, width), row),
                  pl.BlockSpec((seq, xbc.shape[1]), row),
                  pl.BlockSpec((seq, LANES), row),
                  pl.BlockSpec(cw.shape, const), pl.BlockSpec(cb.shape, const),
                  pl.BlockSpec(dtb.shape, const), pl.BlockSpec(alog.shape, const),
                  pl.BlockSpec(dsk.shape, const), pl.BlockSpec(g.shape, const)],
        out_specs=pl.BlockSpec((seq, width), row),
        out_shape=jax.ShapeDtypeStruct((batch * seq, width), F32),
        compiler_params=pltpu.CompilerParams(dimension_semantics=("arbitrary",),
                                             vmem_limit_bytes=VMEM_LIMIT_BYTES),
        name="ssd",
    )(zs, xbc, small, cw, cb, dtb, alog, dsk, g)


def _lru_kernel(x_ref, gg_ref, cw_ref, cb_ref, wa_ref, ba_ref, wx_ref, bx_ref, lam_ref, g_ref, o_ref,
                *, chunk):
    seq = x_ref.shape[0]
    w = o_ref.shape[1]
    sp = _softplus(-lam_ref[...])
    rowi = _iota((chunk, w), 0)

    def body(c, hprev):
        r0 = c * chunk
        prev = x_ref[r0 - 8:r0, :] if c > 0 else jnp.zeros((8, w), F32)
        xl = _causal_conv(prev, x_ref[r0:r0 + chunk, :], cw_ref, cb_ref)
        xb = xl.astype(BF16)
        r = _sigmoid(_dot(xb, wa_ref[...]) + ba_ref[...])
        i = _sigmoid(_dot(xb, wx_ref[...]) + bx_ref[...])
        log_a = (-LRU_C) * r * sp
        a = jnp.exp(log_a)
        u = jnp.sqrt(-jnp.tanh(log_a) * (a * a + 1.0)) * (i * xl)
        d = 1
        while d < chunk:
            if d < 8:
                keep = rowi >= d
                a_sh = jnp.where(keep, pltpu.roll(a, d, 0), 1.0)
                u_sh = jnp.where(keep, pltpu.roll(u, d, 0), 0.0)
            else:
                a_sh = jnp.concatenate([jnp.ones((d, w), F32), a[:chunk - d]], axis=0)
                u_sh = jnp.concatenate([jnp.zeros((d, w), F32), u[:chunk - d]], axis=0)
            u = a * u_sh + u
            a = a * a_sh
            d *= 2
        h = a * hprev + u
        o_ref[pl.ds(r0, chunk), :] = _rms(h * gg_ref[pl.ds(r0, chunk), :], g_ref[...])
        return h[chunk - 1:chunk, :]

    hprev = jnp.zeros((1, w), F32)
    for c in range(seq // chunk):
        hprev = body(c, hprev)


def _lru(x, gg, cw, cb, wa, ba, wx, bx, lam, g, *, batch, seq, chunk=256):
    w = x.shape[1]
    const = lambda b: (0, 0)
    row = lambda b: (b, 0)
    return pl.pallas_call(
        functools.partial(_lru_kernel, chunk=chunk),
        grid=(batch,),
        in_specs=[pl.BlockSpec((seq, w), row), pl.BlockSpec((seq, w), row),
                  pl.BlockSpec(cw.shape, const), pl.BlockSpec(cb.shape, const),
                  pl.BlockSpec(wa.shape, const), pl.BlockSpec(ba.shape, const),
                  pl.BlockSpec(wx.shape, const), pl.BlockSpec(bx.shape, const),
                  pl.BlockSpec(lam.shape, const), pl.BlockSpec(g.shape, const)],
        out_specs=pl.BlockSpec((seq, w), row),
        out_shape=jax.ShapeDtypeStruct((batch * seq, w), F32),
        compiler_params=pltpu.CompilerParams(dimension_semantics=("arbitrary",),
                                             vmem_limit_bytes=VMEM_LIMIT_BYTES),
        name="rglru",
    )(x, gg, cw, cb, wa, ba, wx, bx, lam, g)


FOX_BLOCK = 256
FOX_PART_STRIDE = 16
FOX_VT_ROWS = HEAD_DIM + BF16_SUBLANES


def _fox_bias_matrices(heads):
    pairs = heads * HEAD_DIM // LANES
    assert SMALL_F_COL + heads <= FOX_PART_STRIDE
    pm_q = np.zeros((pairs * LANES, LANES), np.float32)
    pm_k = np.zeros((LANES, pairs * LANES), np.float32)
    ones_q = np.zeros((LANES, 1), np.float32)
    ones_k = np.zeros((1, LANES), np.float32)
    for j in range(pairs):
        for part in range(3):
            ca = part * FOX_PART_STRIDE + SMALL_F_COL + 2 * j
            pm_q[j * LANES + HEAD_DIM + part, ca] = 1.0
            pm_q[j * LANES + part, ca + 1] = 1.0
            pm_k[ca, j * LANES + HEAD_DIM + 3 + part] = -1.0
            pm_k[ca + 1, j * LANES + 3 + part] = -1.0
    for base in (0, HEAD_DIM):
        ones_q[base + 3:base + 6, 0] = 1.0
        ones_k[0, base:base + 3] = 1.0
    return jnp.asarray(pm_q, BF16), jnp.asarray(pm_k, BF16), jnp.asarray(ones_q, F32), jnp.asarray(ones_k, F32)


def _fox_kernel(qt_ref, k_ref, vt_in_ref, small_ref, bf_ref, pmq_ref, pmk_ref, onesq_ref, onesk_ref, g_ref,
                o_ref, qaugt_ref, kaug_ref, vt_ref, *, heads):
    seq = k_ref.shape[0]
    width = heads * HEAD_DIM
    pairs = width // LANES
    blk = FOX_BLOCK

    def build_features():
        tril = (_iota((blk, blk), 0) >= _iota((blk, blk), 1)).astype(BF16)
        f_lane = (_iota((1, LANES), 1) >= SMALL_F_COL) & (_iota((1, LANES), 1) < SMALL_F_COL + heads)
        head_a = _iota((blk, LANES), 1) < HEAD_DIM
        head_a_t = _iota((LANES, blk), 0) < HEAD_DIM
        ones_rows = jnp.where(_iota((BF16_SUBLANES, blk), 0) == 0, 1.0, 0.0).astype(BF16)

        def body(c, carry):
            r0 = c * blk
            sm = small_ref[pl.ds(r0, blk), :]
            log_f = jnp.where(f_lane, -_softplus(-(sm + bf_ref[...])), 0.0)
            cs = _exact_left(tril, log_f) + carry
            hi, mid, lo = _split3(cs * LOG2E)
            packed = (hi.astype(F32) + pltpu.roll(mid.astype(F32), FOX_PART_STRIDE, 1)
                      + pltpu.roll(lo.astype(F32), 2 * FOX_PART_STRIDE, 1)).astype(BF16)
            fq_all = _dot_nt(pmq_ref[...], packed)
            fk_all = _dot(packed, pmk_ref[...])
            q_t = qt_ref[c]
            v_t = vt_in_ref[c]
            for j in range(pairs):
                rows = slice(j * LANES, (j + 1) * LANES)
                fq_t = (fq_all[rows] + onesq_ref[...]).astype(BF16)
                fk = (fk_all[:, rows] + onesk_ref[...]).astype(BF16)
                k2 = k_ref[pl.ds(r0, blk), rows]
                qaugt_ref[2 * j, c] = jnp.where(head_a_t, q_t[rows], fq_t)
                qaugt_ref[2 * j + 1, c] = jnp.where(head_a_t, fq_t, q_t[rows])
                kaug_ref[2 * j, c] = jnp.where(head_a, k2, fk)
                kaug_ref[2 * j + 1, c] = jnp.where(head_a, fk, k2)
                vt_ref[2 * j, c] = jnp.concatenate([v_t[j * LANES:j * LANES + HEAD_DIM], ones_rows], axis=0)
                vt_ref[2 * j + 1, c] = jnp.concatenate([v_t[j * LANES + HEAD_DIM:(j + 1) * LANES], ones_rows],
                                                       axis=0)
            return cs[blk - 1:blk, :]

        carry = jnp.zeros((1, LANES), F32)
        for c in range(seq // blk):
            carry = body(c, carry)

    build_features()
    visible = _iota((blk, blk), 0) <= _iota((blk, blk), 1)

    def logits(h, kb, qi):
        return _dot(kaug_ref[h, kb], qaugt_ref[h, qi])

    def update(h, kb, state, s_t):
        m, acc = state
        mn = jnp.maximum(m, jnp.max(s_t, axis=0, keepdims=True))
        p_t = jnp.exp2(s_t - mn).astype(BF16)
        return mn, acc * jnp.exp2(m - mn) + _dot(vt_ref[h, kb], p_t)

    for qi in range(seq // blk):
        state = [(jnp.full((1, blk), -jnp.inf, F32), jnp.zeros((FOX_VT_ROWS, blk), F32)) for _ in range(heads)]
        nxt = [logits(h, 0, qi) for h in range(heads)]
        for kb in range(qi):
            cur, nxt = nxt, []
            for h in range(heads):
                nxt.append(logits(h, kb + 1, qi))
                state[h] = update(h, kb, state[h], cur[h])
        state = [update(h, qi, state[h], jnp.where(visible, nxt[h], -jnp.inf)) for h in range(heads)]

        outs = []
        for j in range(pairs):
            o_t = jnp.concatenate([state[h][1][:HEAD_DIM] / state[h][1][HEAD_DIM:HEAD_DIM + 1]
                                   for h in (2 * j, 2 * j + 1)], axis=0)
            outs.append(o_t.T)
        o_ref[qi * blk:(qi + 1) * blk, :] = _rms(jnp.concatenate(outs, axis=1), g_ref[...])


def _fox(q_t, k, v_t, small, bf, g, *, batch, seq, heads):
    width = heads * HEAD_DIM
    blk = FOX_BLOCK
    nblk = seq // blk
    pm_q, pm_k, ones_q, ones_k = _fox_bias_matrices(heads)
    const = lambda b: (0, 0)
    return pl.pallas_call(
        functools.partial(_fox_kernel, heads=heads),
        grid=(batch,),
        in_specs=[pl.BlockSpec((nblk, width, blk), lambda b: (b, 0, 0)),
                  pl.BlockSpec((seq, width), lambda b: (b, 0)),
                  pl.BlockSpec((nblk, width, blk), lambda b: (b, 0, 0)),
                  pl.BlockSpec((seq, LANES), lambda b: (b, 0)),
                  pl.BlockSpec(bf.shape, const),
                  pl.BlockSpec(pm_q.shape, const), pl.BlockSpec(pm_k.shape, const),
                  pl.BlockSpec(ones_q.shape, const), pl.BlockSpec(ones_k.shape, const),
                  pl.BlockSpec(g.shape, const)],
        out_specs=pl.BlockSpec((seq, width), lambda b: (b, 0)),
        out_shape=jax.ShapeDtypeStruct((batch * seq, width), F32),
        scratch_shapes=[pltpu.VMEM((heads, nblk, LANES, blk), BF16),
                        pltpu.VMEM((heads, nblk, blk, LANES), BF16),
                        pltpu.VMEM((heads, nblk, FOX_VT_ROWS, blk), BF16)],
        compiler_params=pltpu.CompilerParams(dimension_semantics=("arbitrary",),
                                             vmem_limit_bytes=VMEM_LIMIT_BYTES),
        name="fox",
    )(q_t, k, v_t, small, bf, pm_q, pm_k, ones_q, ones_k, g)


def _mixer_kernel(h_ref, ys_ref, yl_ref, yf_ref, p_ref, wo_ref, g2_ref, wg_ref, wu_ref, wd_ref,
                  g3_ref, wpg_ref, bpg_ref, wpp_ref, gf_ref, o_ref, *, ff_chunk, final_norm):
    y = jnp.concatenate([ys_ref[...], yl_ref[...], yf_ref[...]], axis=1).astype(BF16)
    h = h_ref[...] + _dot(y, wo_ref[...])
    u = _rms(h, g2_ref[...]).astype(BF16)
    d_ff = wg_ref.shape[1]
    for c0 in range(0, d_ff, ff_chunk):
        gt = _dot(u, wg_ref[:, c0:c0 + ff_chunk])
        up = _dot(u, wu_ref[:, c0:c0 + ff_chunk])
        act = (gt * _sigmoid(gt) * up).astype(BF16)
        h = h + _dot(act, wd_ref[c0:c0 + ff_chunk, :])
    u = _rms(h, g3_ref[...]).astype(BF16)
    gate = _sigmoid(_dot(u, wpg_ref[...]) + bpg_ref[...])
    h = h + gate * _dot(p_ref[...].astype(BF16), wpp_ref[...])
    if final_norm:
        h = _rms(h, gf_ref[...])
    o_ref[...] = h


def _mixer(h, ys, yl, yf, p, wo, g2, wg, wu, wd, g3, wpg, bpg, wpp, gf, *, layer, tm, ff_chunk, final_norm):
    t, d = h.shape
    const = lambda i: (0, 0)
    row = lambda i: (i, 0)
    wspec = lambda a: pl.BlockSpec(a.shape, const, pipeline_mode=pl.Buffered(1))
    vspec = lambda a: pl.BlockSpec(a.shape, const)
    return pl.pallas_call(
        functools.partial(_mixer_kernel, ff_chunk=ff_chunk, final_norm=final_norm),
        grid=(t // tm,),
        in_specs=[pl.BlockSpec((tm, d), row),
                  pl.BlockSpec((tm, ys.shape[1]), row),
                  pl.BlockSpec((tm, yl.shape[1]), row),
                  pl.BlockSpec((tm, yf.shape[1]), row),
                  pl.BlockSpec((None, tm, p.shape[2]), lambda i: (layer, i, 0)),
                  wspec(wo), vspec(g2), wspec(wg), wspec(wu), wspec(wd),
                  vspec(g3), wspec(wpg), vspec(bpg), wspec(wpp), vspec(gf)],
        out_specs=pl.BlockSpec((tm, d), row),
        out_shape=jax.ShapeDtypeStruct((t, d), F32),
        compiler_params=pltpu.CompilerParams(dimension_semantics=("arbitrary",),
                                             vmem_limit_bytes=VMEM_LIMIT_BYTES),
        name="mixer",
    )(h, ys, yl, yf, p, wo, g2, wg, wu, wd, g3, wpg, bpg, wpp, gf)


def _pad_lanes(v, col0):
    out = jnp.zeros((1, LANES), F32)
    return out.at[0, col0:col0 + v.shape[0]].set(v.astype(F32))


def _block_diag(w):
    nb, bw, _ = w.shape
    out = jnp.zeros((nb * bw, nb * bw), w.dtype)
    for g in range(nb):
        out = out.at[g * bw:(g + 1) * bw, g * bw:(g + 1) * bw].set(w[g])
    return out


def kernel(x, p, norm1_g, w_in, ssd_conv_w, ssd_conv_b, ssd_dt_bias, ssd_a_log, ssd_d, ssd_norm_g,
           lru_conv_w, lru_conv_b, lru_w_a, lru_b_a, lru_w_x, lru_b_x, lru_lambda, lru_norm_g,
           fox_b_f, fox_norm_g, w_out, norm2_g, w_gate, w_up, w_down, norm3_g, w_ple_gate,
           b_ple_gate, w_ple_proj, final_norm_g):
    batch, seq, d_model = x.shape
    depth = w_in.shape[0]
    ssd_heads = ssd_dt_bias.shape[1]
    ssd_width = ssd_heads * HEAD_DIM
    conv_dim = ssd_conv_w.shape[2]
    ssd_state = (conv_dim - ssd_width) // (2 * SSD_GROUPS)
    lru_width = lru_conv_w.shape[2]
    fox_heads = fox_b_f.shape[1]
    fox_width = fox_heads * HEAD_DIM
    tokens = batch * seq

    o_z = 0
    o_xbc = o_z + ssd_width
    o_dt = o_xbc + conv_dim
    o_lx = o_dt + ssd_heads
    o_q = o_lx + 2 * lru_width
    o_f = o_q + 3 * fox_width
    q_gain = HEAD_DIM ** -0.5 * LOG2E
    outs = ((ssd_width, F32, 1.0, None, "silu"), (conv_dim, F32, 1.0, None, None),
            (lru_width, F32, 1.0, None, None), (lru_width, F32, 1.0, None, "gelu"),
            (fox_width, BF16, q_gain, FOX_BLOCK, None), (fox_width, BF16, 1.0, None, None),
            (fox_width, BF16, 1.0, FOX_BLOCK, None), (LANES, F32, 1.0, None, None))

    h = x.reshape(tokens, d_model)
    row = lambda v: v.reshape(1, -1).astype(F32)
    w_in_rows = w_in.reshape(depth * d_model, -1)
    p_rows = p.reshape(depth, tokens, -1)
    for i in range(depth):
        w = w_in_rows[i * d_model:(i + 1) * d_model]
        w_small = jnp.zeros((d_model, LANES), w.dtype)
        w_small = w_small.at[:, SMALL_DT_COL:SMALL_DT_COL + ssd_heads].set(w[:, o_dt:o_lx])
        w_small = w_small.at[:, SMALL_F_COL:SMALL_F_COL + fox_heads].set(w[:, o_f:o_f + fox_heads])
        w_perm = jnp.concatenate([w[:, o_z:o_dt], w[:, o_lx:o_f], w_small], axis=1).astype(BF16)

        zs, xbc, lru_x, lru_gate, q_t, k, v_t, small = _in_proj(h, row(norm1_g[i]), w_perm, outs, tm=1024)

        y_ssd = _ssd(zs, xbc, small, ssd_conv_w[i], row(ssd_conv_b[i]),
                     _pad_lanes(ssd_dt_bias[i], SMALL_DT_COL), _pad_lanes(ssd_a_log[i], SMALL_DT_COL),
                     row(jnp.repeat(ssd_d[i], HEAD_DIM)), row(ssd_norm_g[i]),
                     batch=batch, seq=seq, heads=ssd_heads, groups=SSD_GROUPS, state=ssd_state)
        y_lru = _lru(lru_x, lru_gate, lru_conv_w[i], row(lru_conv_b[i]),
                     _block_diag(lru_w_a[i]).astype(BF16), row(lru_b_a[i]),
                     _block_diag(lru_w_x[i]).astype(BF16), row(lru_b_x[i]),
                     row(lru_lambda[i]), row(lru_norm_g[i]), batch=batch, seq=seq)
        y_fox = _fox(q_t, k, v_t, small, _pad_lanes(fox_b_f[i], SMALL_F_COL), row(fox_norm_g[i]),
                     batch=batch, seq=seq, heads=fox_heads)
        h = _mixer(h, y_ssd, y_lru, y_fox, p_rows,
                   w_out[i].astype(BF16), row(norm2_g[i]),
                   w_gate[i].astype(BF16), w_up[i].astype(BF16), w_down[i].astype(BF16),
                   row(norm3_g[i]), w_ple_gate[i].astype(BF16), row(b_ple_gate[i]),
                   w_ple_proj[i].astype(BF16), row(final_norm_g),
                   layer=i, tm=512, ff_chunk=256, final_norm=(i == depth - 1))
    return h.reshape(batch, seq, d_model)
```

```python
import functools

import jax
import jax.numpy as jnp
import numpy as np
from jax import lax
from jax.experimental import pallas as pl
from jax.experimental.pallas import tpu as pltpu

F32 = jnp.float32
BF16 = jnp.bfloat16

EPS = 1e-6
HEAD_DIM = 64
LANES = 128
BF16_SUBLANES = 16
MXU_WIDTH = 256
IN_PROJ_SUB = 256
SSD_CHUNK = 128
SSD_GROUPS = 2
LRU_C = 8.0
LRU_SCAN = 8
LOG2E = 1.4426950408889634
SMALL_DT_COL = 0
SMALL_F_COL = 8
VMEM_LIMIT_BYTES = 56 * 1024 * 1024


def _dot(a, b):
    return jnp.dot(a, b, preferred_element_type=F32)


def _dot_nt(a, b):
    return lax.dot_general(a, b, (((1,), (1,)), ((), ())), preferred_element_type=F32)


def _rms(x, g):
    ms = jnp.mean(x * x, axis=-1, keepdims=True)
    return x * lax.rsqrt(ms + EPS) * g


def _sigmoid(x):
    return 1.0 / (1.0 + jnp.exp(-x))


def _softplus(x):
    return jnp.maximum(x, 0.0) + jnp.log1p(jnp.exp(-jnp.abs(x)))


def _split3(x):
    hi = x.astype(BF16)
    r = x - hi.astype(F32)
    mid = r.astype(BF16)
    lo = (r - mid.astype(F32)).astype(BF16)
    return hi, mid, lo


def _exact_left(m, x):
    hi, mid, lo = _split3(x)
    return _dot(m, hi) + _dot(m, mid) + _dot(m, lo)


def _iota(shape, axis):
    return lax.broadcasted_iota(jnp.int32, shape, axis)


def _causal_conv(prev, cur, w_ref, b_ref):
    ext = jnp.concatenate([prev, cur], axis=0)
    k = w_ref.shape[0]
    acc = cur * w_ref[k - 1:k, :] + b_ref[...]
    for j in range(1, k):
        acc = acc + pltpu.roll(ext, j, 0)[8:, :] * w_ref[k - 1 - j:k - j, :]
    return acc


def _silu(x):
    return x * _sigmoid(x)


def _gelu_tanh(x):
    return 0.5 * x * (1.0 + jnp.tanh(0.7978845608028654 * (x + 0.044715 * (x * x * x))))


def _in_proj_kernel(h_ref, g_ref, w_ref, *out_refs, outs, groups, sub):
    nsub = h_ref.shape[0] // sub

    def products(s):
        u = _rms(h_ref[s * sub:(s + 1) * sub, :], g_ref[...]).astype(BF16)
        return [_dot(u, w_ref[:, a:b]) for a, b in groups]

    def finish(s, prods):
        for o_ref, (gi, a, b, gain, tblk, epi) in zip(out_refs, outs):
            y = prods[gi][:, a:b]
            if gain != 1.0:
                y = y * gain
            if epi == "silu":
                y = _silu(y)
            elif epi == "gelu":
                y = _gelu_tanh(y)
            if tblk is None:
                o_ref[s * sub:(s + 1) * sub, :] = y.astype(o_ref.dtype)
            else:
                for t in range(sub // tblk):
                    o_ref[s * (sub // tblk) + t] = y[t * tblk:(t + 1) * tblk, :].T.astype(o_ref.dtype)

    pending = products(0)
    for s in range(1, nsub):
        nxt = products(s)
        finish(s - 1, pending)
        pending = nxt
    finish(nsub - 1, pending)


def _in_proj(h, g, w, outs, tm):
    t, d = h.shape
    specs, shapes, cfg, groups, off, g0 = [], [], [], [], 0, 0
    for wd, dt, gain, tblk, epi in outs:
        cfg.append((len(groups), off - g0, off - g0 + wd, gain, tblk, epi))
        off += wd
        if (off - g0) % MXU_WIDTH == 0:
            groups.append((g0, off))
            g0 = off
        if tblk is None:
            specs.append(pl.BlockSpec((tm, wd), lambda i: (i, 0)))
            shapes.append(jax.ShapeDtypeStruct((t, wd), dt))
        else:
            specs.append(pl.BlockSpec((tm // tblk, wd, tblk), lambda i: (i, 0, 0)))
            shapes.append(jax.ShapeDtypeStruct((t // tblk, wd, tblk), dt))
    assert g0 == off == w.shape[1], "output widths must tile the weight columns in MXU-width groups"
    const = lambda i: (0, 0)
    return pl.pallas_call(
        functools.partial(_in_proj_kernel, outs=tuple(cfg), groups=tuple(groups), sub=IN_PROJ_SUB),
        grid=(t // tm,),
        in_specs=[pl.BlockSpec((tm, d), lambda i: (i, 0)),
                  pl.BlockSpec((1, d), const),
                  pl.BlockSpec(w.shape, const, pipeline_mode=pl.Buffered(1))],
        out_specs=specs,
        out_shape=shapes,
        compiler_params=pltpu.CompilerParams(dimension_semantics=("arbitrary",),
                                             vmem_limit_bytes=VMEM_LIMIT_BYTES),
        name="in_proj",
    )(h, g, w)


def _ssd_kernel(zs_ref, xbc_ref, small_ref, cw_ref, cb_ref, dtb_ref, alog_ref, dsk_ref, g_ref,
                o_ref, *, heads, groups, state):
    seq = xbc_ref.shape[0]
    L = SSD_CHUNK
    width = heads * HEAD_DIM
    pairs = width // LANES
    hpg = heads // groups
    causal = _iota((L, L), 0) >= _iota((L, L), 1)
    tril = causal.astype(BF16)
    head_lane = (_iota((1, LANES), 1) >= SMALL_DT_COL) & (_iota((1, LANES), 1) < SMALL_DT_COL + heads)
    first_half = _iota((L, LANES), 1) < HEAD_DIM
    a_row = jnp.where(head_lane, -jnp.exp(alog_ref[...]), 0.0)
    win_tiles = MXU_WIDTH // LANES
    win0 = [g * hpg * HEAD_DIM // LANES for g in range(groups)]
    assert all((g + 1) * hpg * HEAD_DIM <= (win0[g] + win_tiles) * LANES <= width for g in range(groups))

    def chunk(c, states):
        r0 = c * L
        prev = xbc_ref[r0 - 8:r0, :] if c > 0 else jnp.zeros((8, xbc_ref.shape[1]), F32)
        xc = _silu(_causal_conv(prev, xbc_ref[r0:r0 + L, :], cw_ref, cb_ref))
        bm = xc[:, width:width + groups * state]
        cm_b = xc[:, width + groups * state:].astype(BF16)
        bm_b = bm.astype(BF16)
        sm = small_ref[pl.ds(r0, L), :]
        dt = jnp.where(head_lane, _softplus(sm + dtb_ref[...]), 0.0)
        acs = _exact_left(tril, dt * a_row) * LOG2E
        acs_t = acs.T
        acs_b = [jnp.broadcast_to(acs[:, SMALL_DT_COL + h:SMALL_DT_COL + h + 1], (L, LANES)) for h in range(heads)]
        dt_b = [jnp.broadcast_to(dt[:, SMALL_DT_COL + h:SMALL_DT_COL + h + 1], (L, LANES)) for h in range(heads)]
        cb = [_dot_nt(cm_b[:, g * state:(g + 1) * state], bm_b[:, g * state:(g + 1) * state])
              for g in range(groups)]
        sc = []
        for h in range(heads):
            seg = acs_b[h] - acs_t[SMALL_DT_COL + h:SMALL_DT_COL + h + 1, :]
            lm = jnp.exp2(jnp.where(causal, seg, -jnp.inf))
            sc.append((cb[h // hpg] * lm).astype(BF16))
        xs, xdt, xdw, eacs, cdec, yd = [], [], [], [], [], []
        for j in range(pairs):
            ha, hb = 2 * j, 2 * j + 1
            dt_e = jnp.where(first_half, dt_b[ha], dt_b[hb])
            acs_e = jnp.where(first_half, acs_b[ha], acs_b[hb])
            last_e = acs_e[L - 1:L, :]
            xs.append(xc[:, j * LANES:(j + 1) * LANES])
            xdt.append(xs[j] * dt_e)
            xdw.append((xdt[j] * jnp.exp2(last_e - acs_e)).astype(BF16))
            eacs.append(jnp.exp2(acs_e))
            cdec.append(jnp.exp2(last_e))
            x_bd = jnp.concatenate([jnp.where(first_half, xdt[j], 0.0), jnp.where(first_half, 0.0, xdt[j])],
                                   axis=0).astype(BF16)
            yd.append(_dot(jnp.concatenate([sc[ha], sc[hb]], axis=1), x_bd))
        yo, new_states = [], []
        for g in range(groups):
            tiles = range(win0[g], win0[g] + win_tiles)
            yo.append(_dot(cm_b[:, g * state:(g + 1) * state], states[g].astype(BF16)))
            bm_t = bm[:, g * state:(g + 1) * state].T.astype(BF16)
            new = _dot(bm_t, jnp.concatenate([xdw[t] for t in tiles], axis=1))
            new_states.append(states[g] * jnp.concatenate([cdec[t] for t in tiles], axis=1) + new)
        ys = []
        for j in range(pairs):
            ga, gb = 2 * j // hpg, (2 * j + 1) // hpg
            yo_a = yo[ga][:, (j - win0[ga]) * LANES:(j - win0[ga] + 1) * LANES]
            yo_b = yo[gb][:, (j - win0[gb]) * LANES:(j - win0[gb] + 1) * LANES]
            yo_j = yo_a if ga == gb else jnp.where(first_half, yo_a, yo_b)
            ys.append(yd[j] + yo_j * eacs[j] + xs[j] * dsk_ref[:, j * LANES:(j + 1) * LANES])
        y = jnp.concatenate(ys, axis=1)
        o_ref[pl.ds(r0, L), :] = _rms(y * zs_ref[pl.ds(r0, L), :], g_ref[...])
        return new_states

    states = [jnp.zeros((state, MXU_WIDTH), F32) for _ in range(groups)]
    for c in range(seq // L):
        states = chunk(c, states)


def _ssd(zs, xbc, small, cw, cb, dtb, alog, dsk, g, *, batch, seq, heads, groups, state):
    width = heads * HEAD_DIM
    const = lambda b: (0, 0)
    row = lambda b: (b, 0)
    return pl.pallas_call(
        functools.partial(_ssd_kernel, heads=heads, groups=groups, state=state),
        grid=(batch,),
        in_specs=[pl.BlockSpec((seq, width), row),
                  pl.BlockSpec((seq, xbc.shape[1]), row),
                  pl.BlockSpec((seq, LANES), row),
                  pl.BlockSpec(cw.shape, const), pl.BlockSpec(cb.shape, const),
                  pl.BlockSpec(dtb.shape, const), pl.BlockSpec(alog.shape, const),
                  pl.BlockSpec(dsk.shape, const), pl.BlockSpec(g.shape, const)],
        out_specs=pl.BlockSpec((seq, width), row),
        out_shape=jax.ShapeDtypeStruct((batch * seq, width), F32),
        compiler_params=pltpu.CompilerParams(dimension_semantics=("arbitrary",),
                                             vmem_limit_bytes=VMEM_LIMIT_BYTES),
        name="ssd",
    )(zs, xbc, small, cw, cb, dtb, alog, dsk, g)


def _lru_kernel(x_ref, gg_ref, cw_ref, cb_ref, wa_ref, ba_ref, wx_ref, bx_ref, lam_ref, g_ref, o_ref,
                *, chunk):
    seq = x_ref.shape[0]
    w = o_ref.shape[1]
    sp = _softplus(-lam_ref[...])
    rowi = _iota((LRU_SCAN, w), 0)

    def body(c, hprev):
        r0 = c * chunk
        prev = x_ref[r0 - 8:r0, :] if c > 0 else jnp.zeros((8, w), F32)
        xl = _causal_conv(prev, x_ref[r0:r0 + chunk, :], cw_ref, cb_ref)
        xb = xl.astype(BF16)
        r = _sigmoid(_dot(xb, wa_ref[...]) + ba_ref[...])
        i = _sigmoid(_dot(xb, wx_ref[...]) + bx_ref[...])
        log_a = (-LRU_C) * r * sp
        a = jnp.exp(log_a)
        u = jnp.sqrt(-jnp.tanh(log_a) * (a * a + 1.0)) * (i * xl)
        hs = []
        for s in range(chunk // LRU_SCAN):
            a_s = a[s * LRU_SCAN:(s + 1) * LRU_SCAN]
            u_s = u[s * LRU_SCAN:(s + 1) * LRU_SCAN]
            d = 1
            while d < LRU_SCAN:
                if d < 8:
                    keep = rowi >= d
                    a_sh = jnp.where(keep, pltpu.roll(a_s, d, 0), 1.0)
                    u_sh = jnp.where(keep, pltpu.roll(u_s, d, 0), 0.0)
                else:
                    a_sh = jnp.concatenate([jnp.ones((d, w), F32), a_s[:LRU_SCAN - d]], axis=0)
                    u_sh = jnp.concatenate([jnp.zeros((d, w), F32), u_s[:LRU_SCAN - d]], axis=0)
                u_s = a_s * u_sh + u_s
                a_s = a_s * a_sh
                d *= 2
            h_s = a_s * hprev + u_s
            hprev = h_s[LRU_SCAN - 1:LRU_SCAN, :]
            hs.append(h_s)
        h = jnp.concatenate(hs, axis=0)
        o_ref[pl.ds(r0, chunk), :] = _rms(h * gg_ref[pl.ds(r0, chunk), :], g_ref[...])
        return hprev

    hprev = jnp.zeros((1, w), F32)
    for c in range(seq // chunk):
        hprev = body(c, hprev)


def _lru(x, gg, cw, cb, wa, ba, wx, bx, lam, g, *, batch, seq, chunk=256):
    w = x.shape[1]
    const = lambda b: (0, 0)
    row = lambda b: (b, 0)
    return pl.pallas_call(
        functools.partial(_lru_kernel, chunk=chunk),
        grid=(batch,),
        in_specs=[pl.BlockSpec((seq, w), row), pl.BlockSpec((seq, w), row),
                  pl.BlockSpec(cw.shape, const), pl.BlockSpec(cb.shape, const),
                  pl.BlockSpec(wa.shape, const), pl.BlockSpec(ba.shape, const),
                  pl.BlockSpec(wx.shape, const), pl.BlockSpec(bx.shape, const),
                  pl.BlockSpec(lam.shape, const), pl.BlockSpec(g.shape, const)],
        out_specs=pl.BlockSpec((seq, w), row),
        out_shape=jax.ShapeDtypeStruct((batch * seq, w), F32),
        compiler_params=pltpu.CompilerParams(dimension_semantics=("arbitrary",),
                                             vmem_limit_bytes=VMEM_LIMIT_BYTES),
        name="rglru",
    )(x, gg, cw, cb, wa, ba, wx, bx, lam, g)


FOX_BLOCK = 256
FOX_PART_STRIDE = 16
FOX_VT_ROWS = HEAD_DIM + BF16_SUBLANES


def _fox_bias_matrices(heads):
    pairs = heads * HEAD_DIM // LANES
    assert SMALL_F_COL + heads <= FOX_PART_STRIDE
    pm_q = np.zeros((pairs * LANES, LANES), np.float32)
    pm_k = np.zeros((LANES, pairs * LANES), np.float32)
    ones_q = np.zeros((LANES, 1), np.float32)
    ones_k = np.zeros((1, LANES), np.float32)
    for j in range(pairs):
        for part in range(3):
            ca = part * FOX_PART_STRIDE + SMALL_F_COL + 2 * j
            pm_q[j * LANES + HEAD_DIM + part, ca] = 1.0
            pm_q[j * LANES + part, ca + 1] = 1.0
            pm_k[ca, j * LANES + HEAD_DIM + 3 + part] = -1.0
            pm_k[ca + 1, j * LANES + 3 + part] = -1.0
    for base in (0, HEAD_DIM):
        ones_q[base + 3:base + 6, 0] = 1.0
        ones_k[0, base:base + 3] = 1.0
    return jnp.asarray(pm_q, BF16), jnp.asarray(pm_k, BF16), jnp.asarray(ones_q, F32), jnp.asarray(ones_k, F32)


def _fox_kernel(qt_ref, k_ref, vt_in_ref, small_ref, bf_ref, pmq_ref, pmk_ref, onesq_ref, onesk_ref, g_ref,
                o_ref, qaugt_ref, kaug_ref, vt_ref, *, heads):
    seq = k_ref.shape[0]
    width = heads * HEAD_DIM
    pairs = width // LANES
    blk = FOX_BLOCK

    def build_features():
        tril = (_iota((blk, blk), 0) >= _iota((blk, blk), 1)).astype(BF16)
        f_lane = (_iota((1, LANES), 1) >= SMALL_F_COL) & (_iota((1, LANES), 1) < SMALL_F_COL + heads)
        head_a = _iota((blk, LANES), 1) < HEAD_DIM
        head_a_t = _iota((LANES, blk), 0) < HEAD_DIM
        ones_rows = jnp.where(_iota((BF16_SUBLANES, blk), 0) == 0, 1.0, 0.0).astype(BF16)

        def body(c, carry):
            r0 = c * blk
            sm = small_ref[pl.ds(r0, blk), :]
            log_f = jnp.where(f_lane, -_softplus(-(sm + bf_ref[...])), 0.0)
            cs = _exact_left(tril, log_f) + carry
            hi, mid, lo = _split3(cs * LOG2E)
            packed = (hi.astype(F32) + pltpu.roll(mid.astype(F32), FOX_PART_STRIDE, 1)
                      + pltpu.roll(lo.astype(F32), 2 * FOX_PART_STRIDE, 1)).astype(BF16)
            fq_all = _dot_nt(pmq_ref[...], packed)
            fk_all = _dot(packed, pmk_ref[...])
            q_t = qt_ref[c]
            v_t = vt_in_ref[c]
            for j in range(pairs):
                rows = slice(j * LANES, (j + 1) * LANES)
                fq_t = (fq_all[rows] + onesq_ref[...]).astype(BF16)
                fk = (fk_all[:, rows] + onesk_ref[...]).astype(BF16)
                k2 = k_ref[pl.ds(r0, blk), rows]
                qaugt_ref[2 * j, c] = jnp.where(head_a_t, q_t[rows], fq_t)
                qaugt_ref[2 * j + 1, c] = jnp.where(head_a_t, fq_t, q_t[rows])
                kaug_ref[2 * j, c] = jnp.where(head_a, k2, fk)
                kaug_ref[2 * j + 1, c] = jnp.where(head_a, fk, k2)
                vt_ref[2 * j, c] = jnp.concatenate([v_t[j * LANES:j * LANES + HEAD_DIM], ones_rows], axis=0)
                vt_ref[2 * j + 1, c] = jnp.concatenate([v_t[j * LANES + HEAD_DIM:(j + 1) * LANES], ones_rows],
                                                       axis=0)
            return cs[blk - 1:blk, :]

        carry = jnp.zeros((1, LANES), F32)
        for c in range(seq // blk):
            carry = body(c, carry)

    build_features()
    visible = _iota((blk, blk), 0) <= _iota((blk, blk), 1)

    def logits(h, kb, qi):
        return _dot(kaug_ref[h, kb], qaugt_ref[h, qi])

    def update(h, kb, state, s_t):
        m, acc = state
        mn = jnp.maximum(m, jnp.max(s_t, axis=0, keepdims=True))
        p_t = jnp.exp2(s_t - mn).astype(BF16)
        return mn, acc * jnp.exp2(m - mn) + _dot(vt_ref[h, kb], p_t)

    for qi in range(seq // blk):
        state = [(jnp.full((1, blk), -jnp.inf, F32), jnp.zeros((FOX_VT_ROWS, blk), F32)) for _ in range(heads)]
        nxt = [logits(h, 0, qi) for h in range(heads)]
        for kb in range(qi):
            cur, nxt = nxt, []
            for h in range(heads):
                nxt.append(logits(h, kb + 1, qi))
                state[h] = update(h, kb, state[h], cur[h])
        state = [update(h, qi, state[h], jnp.where(visible, nxt[h], -jnp.inf)) for h in range(heads)]

        outs = []
        for j in range(pairs):
            o_t = jnp.concatenate([state[h][1][:HEAD_DIM] / state[h][1][HEAD_DIM:HEAD_DIM + 1]
                                   for h in (2 * j, 2 * j + 1)], axis=0)
            outs.append(o_t.T)
        o_ref[qi * blk:(qi + 1) * blk, :] = _rms(jnp.concatenate(outs, axis=1), g_ref[...])


def _fox(q_t, k, v_t, small, bf, g, *, batch, seq, heads):
    width = heads * HEAD_DIM
    blk = FOX_BLOCK
    nblk = seq // blk
    pm_q, pm_k, ones_q, ones_k = _fox_bias_matrices(heads)
    const = lambda b: (0, 0)
    return pl.pallas_call(
        functools.partial(_fox_kernel, heads=heads),
        grid=(batch,),
        in_specs=[pl.BlockSpec((nblk, width, blk), lambda b: (b, 0, 0)),
                  pl.BlockSpec((seq, width), lambda b: (b, 0)),
                  pl.BlockSpec((nblk, width, blk), lambda b: (b, 0, 0)),
                  pl.BlockSpec((seq, LANES), lambda b: (b, 0)),
                  pl.BlockSpec(bf.shape, const),
                  pl.BlockSpec(pm_q.shape, const), pl.BlockSpec(pm_k.shape, const),
                  pl.BlockSpec(ones_q.shape, const), pl.BlockSpec(ones_k.shape, const),
                  pl.BlockSpec(g.shape, const)],
        out_specs=pl.BlockSpec((seq, width), lambda b: (b, 0)),
        out_shape=jax.ShapeDtypeStruct((batch * seq, width), F32),
        scratch_shapes=[pltpu.VMEM((heads, nblk, LANES, blk), BF16),
                        pltpu.VMEM((heads, nblk, blk, LANES), BF16),
                        pltpu.VMEM((heads, nblk, FOX_VT_ROWS, blk), BF16)],
        compiler_params=pltpu.CompilerParams(dimension_semantics=("arbitrary",),
                                             vmem_limit_bytes=VMEM_LIMIT_BYTES),
        name="fox",
    )(q_t, k, v_t, small, bf, pm_q, pm_k, ones_q, ones_k, g)


def _mixer_kernel(h_ref, ys_ref, yl_ref, yf_ref, p_ref, wo_ref, g2_ref, wg_ref, wu_ref, wd_ref,
                  g3_ref, wpg_ref, bpg_ref, wpp_ref, gf_ref, o_ref, *, ff_chunk, final_norm):
    y = jnp.concatenate([ys_ref[...], yl_ref[...], yf_ref[...]], axis=1).astype(BF16)
    h = h_ref[...] + _dot(y, wo_ref[...])
    u = _rms(h, g2_ref[...]).astype(BF16)
    d_ff = wg_ref.shape[1]
    for c0 in range(0, d_ff, ff_chunk):
        gt = _dot(u, wg_ref[:, c0:c0 + ff_chunk])
        up = _dot(u, wu_ref[:, c0:c0 + ff_chunk])
        act = (_silu(gt) * up).astype(BF16)
        h = h + _dot(act, wd_ref[c0:c0 + ff_chunk, :])
    u = _rms(h, g3_ref[...]).astype(BF16)
    gate = _sigmoid(_dot(u, wpg_ref[...]) + bpg_ref[...])
    h = h + gate * _dot(p_ref[...].astype(BF16), wpp_ref[...])
    if final_norm:
        h = _rms(h, gf_ref[...])
    o_ref[...] = h


def _mixer(h, ys, yl, yf, p, wo, g2, wg, wu, wd, g3, wpg, bpg, wpp, gf, *, layer, tm, ff_chunk, final_norm):
    t, d = h.shape
    const = lambda i: (0, 0)
    row = lambda i: (i, 0)
    wspec = lambda a: pl.BlockSpec(a.shape, const, pipeline_mode=pl.Buffered(1))
    vspec = lambda a: pl.BlockSpec(a.shape, const)
    return pl.pallas_call(
        functools.partial(_mixer_kernel, ff_chunk=ff_chunk, final_norm=final_norm),
        grid=(t // tm,),
        in_specs=[pl.BlockSpec((tm, d), row),
                  pl.BlockSpec((tm, ys.shape[1]), row),
                  pl.BlockSpec((tm, yl.shape[1]), row),
                  pl.BlockSpec((tm, yf.shape[1]), row),
                  pl.BlockSpec((None, tm, p.shape[2]), lambda i: (layer, i, 0)),
                  wspec(wo), vspec(g2), wspec(wg), wspec(wu), wspec(wd),
                  vspec(g3), wspec(wpg), vspec(bpg), wspec(wpp), vspec(gf)],
        out_specs=pl.BlockSpec((tm, d), row),
        out_shape=jax.ShapeDtypeStruct((t, d), F32),
        compiler_params=pltpu.CompilerParams(dimension_semantics=("arbitrary",),
                                             vmem_limit_bytes=VMEM_LIMIT_BYTES),
        name="mixer",
    )(h, ys, yl, yf, p, wo, g2, wg, wu, wd, g3, wpg, bpg, wpp, gf)


def _pad_lanes(v, col0):
    out = jnp.zeros((1, LANES), F32)
    return out.at[0, col0:col0 + v.shape[0]].set(v.astype(F32))


def _block_diag(w):
    nb, bw, _ = w.shape
    out = jnp.zeros((nb * bw, nb * bw), w.dtype)
    for g in range(nb):
        out = out.at[g * bw:(g + 1) * bw, g * bw:(g + 1) * bw].set(w[g])
    return out


def kernel(x, p, norm1_g, w_in, ssd_conv_w, ssd_conv_b, ssd_dt_bias, ssd_a_log, ssd_d, ssd_norm_g,
           lru_conv_w, lru_conv_b, lru_w_a, lru_b_a, lru_w_x, lru_b_x, lru_lambda, lru_norm_g,
           fox_b_f, fox_norm_g, w_out, norm2_g, w_gate, w_up, w_down, norm3_g, w_ple_gate,
           b_ple_gate, w_ple_proj, final_norm_g):
    batch, seq, d_model = x.shape
    depth = w_in.shape[0]
    ssd_heads = ssd_dt_bias.shape[1]
    ssd_width = ssd_heads * HEAD_DIM
    conv_dim = ssd_conv_w.shape[2]
    ssd_state = (conv_dim - ssd_width) // (2 * SSD_GROUPS)
    lru_width = lru_conv_w.shape[2]
    fox_heads = fox_b_f.shape[1]
    fox_width = fox_heads * HEAD_DIM
    tokens = batch * seq

    o_z = 0
    o_xbc = o_z + ssd_width
    o_dt = o_xbc + conv_dim
    o_lx = o_dt + ssd_heads
    o_q = o_lx + 2 * lru_width
    o_f = o_q + 3 * fox_width
    q_gain = HEAD_DIM ** -0.5 * LOG2E
    outs = ((ssd_width, F32, 1.0, None, "silu"), (conv_dim, F32, 1.0, None, None),
            (lru_width, F32, 1.0, None, None), (lru_width, F32, 1.0, None, "gelu"),
            (fox_width, BF16, q_gain, FOX_BLOCK, None), (fox_width, BF16, 1.0, None, None),
            (fox_width, BF16, 1.0, FOX_BLOCK, None), (LANES, F32, 1.0, None, None))

    h = x.reshape(tokens, d_model)
    row = lambda v: v.reshape(1, -1).astype(F32)
    w_in_rows = w_in.reshape(depth * d_model, -1)
    p_rows = p.reshape(depth, tokens, -1)
    for i in range(depth):
        w = w_in_rows[i * d_model:(i + 1) * d_model]
        w_small = jnp.zeros((d_model, LANES), w.dtype)
        w_small = w_small.at[:, SMALL_DT_COL:SMALL_DT_COL + ssd_heads].set(w[:, o_dt:o_lx])
        w_small = w_small.at[:, SMALL_F_COL:SMALL_F_COL + fox_heads].set(w[:, o_f:o_f + fox_heads])
        w_perm = jnp.concatenate([w[:, o_z:o_dt], w[:, o_lx:o_f], w_small], axis=1).astype(BF16)

        zs, xbc, lru_x, lru_gate, q_t, k, v_t, small = _in_proj(h, row(norm1_g[i]), w_perm, outs, tm=1024)

        y_ssd = _ssd(zs, xbc, small, ssd_conv_w[i], row(ssd_conv_b[i]),
                     _pad_lanes(ssd_dt_bias[i], SMALL_DT_COL), _pad_lanes(ssd_a_log[i], SMALL_DT_COL),
                     row(jnp.repeat(ssd_d[i], HEAD_DIM)), row(ssd_norm_g[i]),
                     batch=batch, seq=seq, heads=ssd_heads, groups=SSD_GROUPS, state=ssd_state)
        y_lru = _lru(lru_x, lru_gate, lru_conv_w[i], row(lru_conv_b[i]),
                     _block_diag(lru_w_a[i]).astype(BF16), row(lru_b_a[i]),
                     _block_diag(lru_w_x[i]).astype(BF16), row(lru_b_x[i]),
                     row(lru_lambda[i]), row(lru_norm_g[i]), batch=batch, seq=seq)
        y_fox = _fox(q_t, k, v_t, small, _pad_lanes(fox_b_f[i], SMALL_F_COL), row(fox_norm_g[i]),
                     batch=batch, seq=seq, heads=fox_heads)
        h = _mixer(h, y_ssd, y_lru, y_fox, p_rows,
                   w_out[i].astype(BF16), row(norm2_g[i]),
                   w_gate[i].astype(BF16), w_up[i].astype(BF16), w_down[i].astype(BF16),
                   row(norm3_g[i]), w_ple_gate[i].astype(BF16), row(b_ple_gate[i]),
                   w_ple_proj[i].astype(BF16), row(final_norm_g),
                   layer=i, tm=512, ff_chunk=256, final_norm=(i == depth - 1))
    return h.reshape(batch, seq, d_model)
```

```python
import functools

import jax
import jax.numpy as jnp
import numpy as np
from jax import lax
from jax.experimental import pallas as pl
from jax.experimental.pallas import tpu as pltpu

F32 = jnp.float32
BF16 = jnp.bfloat16

EPS = 1e-6
HEAD_DIM = 64
LANES = 128
BF16_SUBLANES = 16
MXU_WIDTH = 256
IN_PROJ_SUB = 256
SSD_CHUNK = 128
SSD_GROUPS = 2
LRU_C = 8.0
LRU_SCAN = 8
LOG2E = 1.4426950408889634
SMALL_DT_COL = 0
SMALL_F_COL = 8
VMEM_LIMIT_BYTES = 56 * 1024 * 1024


def _dot(a, b):
    return jnp.dot(a, b, preferred_element_type=F32)


def _dot_nt(a, b):
    return lax.dot_general(a, b, (((1,), (1,)), ((), ())), preferred_element_type=F32)


def _rms(x, g):
    ms = jnp.mean(x * x, axis=-1, keepdims=True)
    return x * lax.rsqrt(ms + EPS) * g


def _sigmoid(x):
    return 1.0 / (1.0 + jnp.exp(-x))


def _softplus(x):
    return jnp.maximum(x, 0.0) + jnp.log1p(jnp.exp(-jnp.abs(x)))


def _split3(x):
    hi = x.astype(BF16)
    r = x - hi.astype(F32)
    mid = r.astype(BF16)
    lo = (r - mid.astype(F32)).astype(BF16)
    return hi, mid, lo


def _exact_left(m, x):
    hi, mid, lo = _split3(x)
    return _dot(m, hi) + _dot(m, mid) + _dot(m, lo)


def _iota(shape, axis):
    return lax.broadcasted_iota(jnp.int32, shape, axis)


def _causal_conv(prev, cur, w_ref, b_ref):
    ext = jnp.concatenate([prev, cur], axis=0)
    k = w_ref.shape[0]
    acc = cur * w_ref[k - 1:k, :] + b_ref[...]
    for j in range(1, k):
        acc = acc + pltpu.roll(ext, j, 0)[8:, :] * w_ref[k - 1 - j:k - j, :]
    return acc


def _silu(x):
    return x * _sigmoid(x)


def _gelu_tanh(x):
    return 0.5 * x * (1.0 + jnp.tanh(0.7978845608028654 * (x + 0.044715 * (x * x * x))))


def _in_proj_kernel(h_ref, g_ref, w_ref, *out_refs, outs, groups, sub):
    nsub = h_ref.shape[0] // sub

    def products(s):
        u = _rms(h_ref[s * sub:(s + 1) * sub, :], g_ref[...]).astype(BF16)
        return [_dot(u, w_ref[:, a:b]) for a, b in groups]

    def finish(s, prods):
        for o_ref, (gi, a, b, gain, tblk, epi) in zip(out_refs, outs):
            y = prods[gi][:, a:b]
            if gain != 1.0:
                y = y * gain
            if epi == "silu":
                y = _silu(y)
            elif epi == "gelu":
                y = _gelu_tanh(y)
            if tblk is None:
                o_ref[s * sub:(s + 1) * sub, :] = y.astype(o_ref.dtype)
            else:
                for t in range(sub // tblk):
                    o_ref[s * (sub // tblk) + t] = y[t * tblk:(t + 1) * tblk, :].T.astype(o_ref.dtype)

    pending = products(0)
    for s in range(1, nsub):
        nxt = products(s)
        finish(s - 1, pending)
        pending = nxt
    finish(nsub - 1, pending)


def _in_proj(h, g, w, outs, tm):
    t, d = h.shape
    specs, shapes, cfg, groups, off, g0 = [], [], [], [], 0, 0
    for wd, dt, gain, tblk, epi in outs:
        cfg.append((len(groups), off - g0, off - g0 + wd, gain, tblk, epi))
        off += wd
        if (off - g0) % MXU_WIDTH == 0:
            groups.append((g0, off))
            g0 = off
        if tblk is None:
            specs.append(pl.BlockSpec((tm, wd), lambda i: (i, 0)))
            shapes.append(jax.ShapeDtypeStruct((t, wd), dt))
        else:
            specs.append(pl.BlockSpec((tm // tblk, wd, tblk), lambda i: (i, 0, 0)))
            shapes.append(jax.ShapeDtypeStruct((t // tblk, wd, tblk), dt))
    assert g0 == off == w.shape[1], "output widths must tile the weight columns in MXU-width groups"
    const = lambda i: (0, 0)
    return pl.pallas_call(
        functools.partial(_in_proj_kernel, outs=tuple(cfg), groups=tuple(groups), sub=IN_PROJ_SUB),
        grid=(t // tm,),
        in_specs=[pl.BlockSpec((tm, d), lambda i: (i, 0)),
                  pl.BlockSpec((1, d), const),
                  pl.BlockSpec(w.shape, const, pipeline_mode=pl.Buffered(1))],
        out_specs=specs,
        out_shape=shapes,
        compiler_params=pltpu.CompilerParams(dimension_semantics=("arbitrary",),
                                             vmem_limit_bytes=VMEM_LIMIT_BYTES),
        name="in_proj",
    )(h, g, w)


def _ssd_kernel(zs_ref, xbc_ref, small_ref, cw_ref, cb_ref, dtb_ref, alog_ref, dsk_ref, g_ref,
                o_ref, *, heads, groups, state):
    seq = xbc_ref.shape[0]
    L = SSD_CHUNK
    width = heads * HEAD_DIM
    pairs = width // LANES
    hpg = heads // groups
    causal = _iota((L, L), 0) >= _iota((L, L), 1)
    tril = causal.astype(BF16)
    head_lane = (_iota((1, LANES), 1) >= SMALL_DT_COL) & (_iota((1, LANES), 1) < SMALL_DT_COL + heads)
    first_half = _iota((L, LANES), 1) < HEAD_DIM
    a_row = jnp.where(head_lane, -jnp.exp(alog_ref[...]), 0.0)
    win_tiles = MXU_WIDTH // LANES
    win0 = [g * hpg * HEAD_DIM // LANES for g in range(groups)]
    assert all((g + 1) * hpg * HEAD_DIM <= (win0[g] + win_tiles) * LANES <= width for g in range(groups))

    def chunk(c, states):
        r0 = c * L
        prev = xbc_ref[r0 - 8:r0, :] if c > 0 else jnp.zeros((8, xbc_ref.shape[1]), F32)
        xc = _silu(_causal_conv(prev, xbc_ref[r0:r0 + L, :], cw_ref, cb_ref))
        bm = xc[:, width:width + groups * state]
        cm_b = xc[:, width + groups * state:].astype(BF16)
        bm_b = bm.astype(BF16)
        sm = small_ref[pl.ds(r0, L), :]
        dt = jnp.where(head_lane, _softplus(sm + dtb_ref[...]), 0.0)
        acs = _exact_left(tril, dt * a_row) * LOG2E
        acs_t = acs.T
        acs_b = [jnp.broadcast_to(acs[:, SMALL_DT_COL + h:SMALL_DT_COL + h + 1], (L, LANES)) for h in range(heads)]
        dt_b = [jnp.broadcast_to(dt[:, SMALL_DT_COL + h:SMALL_DT_COL + h + 1], (L, LANES)) for h in range(heads)]
        cb = [_dot_nt(cm_b[:, g * state:(g + 1) * state], bm_b[:, g * state:(g + 1) * state])
              for g in range(groups)]
        sc = []
        for h in range(heads):
            seg = acs_b[h] - acs_t[SMALL_DT_COL + h:SMALL_DT_COL + h + 1, :]
            lm = jnp.exp2(jnp.where(causal, seg, -jnp.inf))
            sc.append((cb[h // hpg] * lm).astype(BF16))
        xs, xdt, xdw, eacs, cdec, yd = [], [], [], [], [], []
        for j in range(pairs):
            ha, hb = 2 * j, 2 * j + 1
            dt_e = jnp.where(first_half, dt_b[ha], dt_b[hb])
            acs_e = jnp.where(first_half, acs_b[ha], acs_b[hb])
            last_e = acs_e[L - 1:L, :]
            xs.append(xc[:, j * LANES:(j + 1) * LANES])
            xdt.append(xs[j] * dt_e)
            xdw.append((xdt[j] * jnp.exp2(last_e - acs_e)).astype(BF16))
            eacs.append(jnp.exp2(acs_e))
            cdec.append(jnp.exp2(last_e))
            x_bd = jnp.concatenate([jnp.where(first_half, xdt[j], 0.0), jnp.where(first_half, 0.0, xdt[j])],
                                   axis=0).astype(BF16)
            yd.append(_dot(jnp.concatenate([sc[ha], sc[hb]], axis=1), x_bd))
        yo, new_states = [], []
        for g in range(groups):
            tiles = range(win0[g], win0[g] + win_tiles)
            yo.append(_dot(cm_b[:, g * state:(g + 1) * state], states[g].astype(BF16)))
            bm_t = bm[:, g * state:(g + 1) * state].T.astype(BF16)
            new = _dot(bm_t, jnp.concatenate([xdw[t] for t in tiles], axis=1))
            new_states.append(states[g] * jnp.concatenate([cdec[t] for t in tiles], axis=1) + new)
        ys = []
        for j in range(pairs):
            ga, gb = 2 * j // hpg, (2 * j + 1) // hpg
            yo_a = yo[ga][:, (j - win0[ga]) * LANES:(j - win0[ga] + 1) * LANES]
            yo_b = yo[gb][:, (j - win0[gb]) * LANES:(j - win0[gb] + 1) * LANES]
            yo_j = yo_a if ga == gb else jnp.where(first_half, yo_a, yo_b)
            ys.append(yd[j] + yo_j * eacs[j] + xs[j] * dsk_ref[:, j * LANES:(j + 1) * LANES])
        y = jnp.concatenate(ys, axis=1)
        o_ref[pl.ds(r0, L), :] = _rms(y * zs_ref[pl.ds(r0, L), :], g_ref[...]).astype(o_ref.dtype)
        return new_states

    states = [jnp.zeros((state, MXU_WIDTH), F32) for _ in range(groups)]
    for c in range(seq // L):
        states = chunk(c, states)


def _ssd(zs, xbc, small, cw, cb, dtb, alog, dsk, g, *, batch, seq, heads, groups, state):
    width = heads * HEAD_DIM
    const = lambda b: (0, 0)
    row = lambda b: (b, 0)
    return pl.pallas_call(
        functools.partial(_ssd_kernel, heads=heads, groups=groups, state=state),
        grid=(batch,),
        in_specs=[pl.BlockSpec((seq, width), row),
                  pl.BlockSpec((seq, xbc.shape[1]), row),
                  pl.BlockSpec((seq, LANES), row),
                  pl.BlockSpec(cw.shape, const), pl.BlockSpec(cb.shape, const),
                  pl.BlockSpec(dtb.shape, const), pl.BlockSpec(alog.shape, const),
                  pl.BlockSpec(dsk.shape, const), pl.BlockSpec(g.shape, const)],
        out_specs=pl.BlockSpec((seq, width), row),
        out_shape=jax.ShapeDtypeStruct((batch * seq, width), BF16),
        compiler_params=pltpu.CompilerParams(dimension_semantics=("arbitrary",),
                                             vmem_limit_bytes=VMEM_LIMIT_BYTES),
        name="ssd",
    )(zs, xbc, small, cw, cb, dtb, alog, dsk, g)


def _lru_kernel(x_ref, gg_ref, cw_ref, cb_ref, wa_ref, ba_ref, wx_ref, bx_ref, lam_ref, g_ref, o_ref,
                *, chunk):
    seq = x_ref.shape[0]
    w = o_ref.shape[1]
    sp = _softplus(-lam_ref[...])
    rowi = _iota((LRU_SCAN, w), 0)

    def body(c, hprev):
        r0 = c * chunk
        prev = x_ref[r0 - 8:r0, :] if c > 0 else jnp.zeros((8, w), F32)
        xl = _causal_conv(prev, x_ref[r0:r0 + chunk, :], cw_ref, cb_ref)
        xb = xl.astype(BF16)
        r = _sigmoid(_dot(xb, wa_ref[...]) + ba_ref[...])
        i = _sigmoid(_dot(xb, wx_ref[...]) + bx_ref[...])
        log_a = (-LRU_C) * r * sp
        a = jnp.exp(log_a)
        u = jnp.sqrt(-jnp.tanh(log_a) * (a * a + 1.0)) * (i * xl)
        hs = []
        for s in range(chunk // LRU_SCAN):
            a_s = a[s * LRU_SCAN:(s + 1) * LRU_SCAN]
            u_s = u[s * LRU_SCAN:(s + 1) * LRU_SCAN]
            d = 1
            while d < LRU_SCAN:
                if d < 8:
                    keep = rowi >= d
                    a_sh = jnp.where(keep, pltpu.roll(a_s, d, 0), 1.0)
                    u_sh = jnp.where(keep, pltpu.roll(u_s, d, 0), 0.0)
                else:
                    a_sh = jnp.concatenate([jnp.ones((d, w), F32), a_s[:LRU_SCAN - d]], axis=0)
                    u_sh = jnp.concatenate([jnp.zeros((d, w), F32), u_s[:LRU_SCAN - d]], axis=0)
                u_s = a_s * u_sh + u_s
                a_s = a_s * a_sh
                d *= 2
            h_s = a_s * hprev + u_s
            hprev = h_s[LRU_SCAN - 1:LRU_SCAN, :]
            hs.append(h_s)
        h = jnp.concatenate(hs, axis=0)
        o_ref[pl.ds(r0, chunk), :] = _rms(h * gg_ref[pl.ds(r0, chunk), :], g_ref[...]).astype(o_ref.dtype)
        return hprev

    hprev = jnp.zeros((1, w), F32)
    for c in range(seq // chunk):
        hprev = body(c, hprev)


def _lru(x, gg, cw, cb, wa, ba, wx, bx, lam, g, *, batch, seq, chunk=256):
    w = x.shape[1]
    const = lambda b: (0, 0)
    row = lambda b: (b, 0)
    return pl.pallas_call(
        functools.partial(_lru_kernel, chunk=chunk),
        grid=(batch,),
        in_specs=[pl.BlockSpec((seq, w), row), pl.BlockSpec((seq, w), row),
                  pl.BlockSpec(cw.shape, const), pl.BlockSpec(cb.shape, const),
                  pl.BlockSpec(wa.shape, const), pl.BlockSpec(ba.shape, const),
                  pl.BlockSpec(wx.shape, const), pl.BlockSpec(bx.shape, const),
                  pl.BlockSpec(lam.shape, const), pl.BlockSpec(g.shape, const)],
        out_specs=pl.BlockSpec((seq, w), row),
        out_shape=jax.ShapeDtypeStruct((batch * seq, w), BF16),
        compiler_params=pltpu.CompilerParams(dimension_semantics=("arbitrary",),
                                             vmem_limit_bytes=VMEM_LIMIT_BYTES),
        name="rglru",
    )(x, gg, cw, cb, wa, ba, wx, bx, lam, g)


FOX_BLOCK = 256
FOX_PART_STRIDE = 16
FOX_VT_ROWS = HEAD_DIM + BF16_SUBLANES


def _fox_bias_matrices(heads):
    pairs = heads * HEAD_DIM // LANES
    assert SMALL_F_COL + heads <= FOX_PART_STRIDE
    pm_q = np.zeros((pairs * LANES, LANES), np.float32)
    pm_k = np.zeros((LANES, pairs * LANES), np.float32)
    ones_q = np.zeros((LANES, 1), np.float32)
    ones_k = np.zeros((1, LANES), np.float32)
    for j in range(pairs):
        for part in range(3):
            ca = part * FOX_PART_STRIDE + SMALL_F_COL + 2 * j
            pm_q[j * LANES + HEAD_DIM + part, ca] = 1.0
            pm_q[j * LANES + part, ca + 1] = 1.0
            pm_k[ca, j * LANES + HEAD_DIM + 3 + part] = -1.0
            pm_k[ca + 1, j * LANES + 3 + part] = -1.0
    for base in (0, HEAD_DIM):
        ones_q[base + 3:base + 6, 0] = 1.0
        ones_k[0, base:base + 3] = 1.0
    return jnp.asarray(pm_q, BF16), jnp.asarray(pm_k, BF16), jnp.asarray(ones_q, F32), jnp.asarray(ones_k, F32)


def _fox_kernel(qt_ref, k_ref, vt_in_ref, small_ref, bf_ref, pmq_ref, pmk_ref, onesq_ref, onesk_ref, g_ref,
                o_ref, qaugt_ref, kaug_ref, vt_ref, *, heads):
    seq = k_ref.shape[0]
    width = heads * HEAD_DIM
    pairs = width // LANES
    blk = FOX_BLOCK

    def build_features():
        tril = (_iota((blk, blk), 0) >= _iota((blk, blk), 1)).astype(BF16)
        f_lane = (_iota((1, LANES), 1) >= SMALL_F_COL) & (_iota((1, LANES), 1) < SMALL_F_COL + heads)
        head_a = _iota((blk, LANES), 1) < HEAD_DIM
        head_a_t = _iota((LANES, blk), 0) < HEAD_DIM
        ones_rows = jnp.where(_iota((BF16_SUBLANES, blk), 0) == 0, 1.0, 0.0).astype(BF16)

        def body(c, carry):
            r0 = c * blk
            sm = small_ref[pl.ds(r0, blk), :]
            log_f = jnp.where(f_lane, -_softplus(-(sm + bf_ref[...])), 0.0)
            cs = _exact_left(tril, log_f) + carry
            hi, mid, lo = _split3(cs * LOG2E)
            packed = (hi.astype(F32) + pltpu.roll(mid.astype(F32), FOX_PART_STRIDE, 1)
                      + pltpu.roll(lo.astype(F32), 2 * FOX_PART_STRIDE, 1)).astype(BF16)
            fq_all = _dot_nt(pmq_ref[...], packed)
            fk_all = _dot(packed, pmk_ref[...])
            q_t = qt_ref[c]
            v_t = vt_in_ref[c]
            for j in range(pairs):
                rows = slice(j * LANES, (j + 1) * LANES)
                fq_t = (fq_all[rows] + onesq_ref[...]).astype(BF16)
                fk = (fk_all[:, rows] + onesk_ref[...]).astype(BF16)
                k2 = k_ref[pl.ds(r0, blk), rows]
                qaugt_ref[2 * j, c] = jnp.where(head_a_t, q_t[rows], fq_t)
                qaugt_ref[2 * j + 1, c] = jnp.where(head_a_t, fq_t, q_t[rows])
                kaug_ref[2 * j, c] = jnp.where(head_a, k2, fk)
                kaug_ref[2 * j + 1, c] = jnp.where(head_a, fk, k2)
                vt_ref[2 * j, c] = jnp.concatenate([v_t[j * LANES:j * LANES + HEAD_DIM], ones_rows], axis=0)
                vt_ref[2 * j + 1, c] = jnp.concatenate([v_t[j * LANES + HEAD_DIM:(j + 1) * LANES], ones_rows],
                                                       axis=0)
            return cs[blk - 1:blk, :]

        carry = jnp.zeros((1, LANES), F32)
        for c in range(seq // blk):
            carry = body(c, carry)

    build_features()
    visible = _iota((blk, blk), 0) <= _iota((blk, blk), 1)

    def logits(h, kb, qi):
        return _dot(kaug_ref[h, kb], qaugt_ref[h, qi])

    def update(h, kb, state, s_t):
        m, acc = state
        mn = jnp.maximum(m, jnp.max(s_t, axis=0, keepdims=True))
        p_t = jnp.exp2(s_t - mn).astype(BF16)
        return mn, acc * jnp.exp2(m - mn) + _dot(vt_ref[h, kb], p_t)

    for qi in range(seq // blk):
        state = [(jnp.full((1, blk), -jnp.inf, F32), jnp.zeros((FOX_VT_ROWS, blk), F32)) for _ in range(heads)]
        nxt = [logits(h, 0, qi) for h in range(heads)]
        for kb in range(qi):
            cur, nxt = nxt, []
            for h in range(heads):
                nxt.append(logits(h, kb + 1, qi))
                state[h] = update(h, kb, state[h], cur[h])
        state = [update(h, qi, state[h], jnp.where(visible, nxt[h], -jnp.inf)) for h in range(heads)]

        outs = []
        for j in range(pairs):
            o_t = jnp.concatenate([state[h][1][:HEAD_DIM] / state[h][1][HEAD_DIM:HEAD_DIM + 1]
                                   for h in (2 * j, 2 * j + 1)], axis=0)
            outs.append(o_t.T)
        o_ref[qi * blk:(qi + 1) * blk, :] = _rms(jnp.concatenate(outs, axis=1), g_ref[...]).astype(o_ref.dtype)


def _fox(q_t, k, v_t, small, bf, g, *, batch, seq, heads):
    width = heads * HEAD_DIM
    blk = FOX_BLOCK
    nblk = seq // blk
    pm_q, pm_k, ones_q, ones_k = _fox_bias_matrices(heads)
    const = lambda b: (0, 0)
    return pl.pallas_call(
        functools.partial(_fox_kernel, heads=heads),
        grid=(batch,),
        in_specs=[pl.BlockSpec((nblk, width, blk), lambda b: (b, 0, 0)),
                  pl.BlockSpec((seq, width), lambda b: (b, 0)),
                  pl.BlockSpec((nblk, width, blk), lambda b: (b, 0, 0)),
                  pl.BlockSpec((seq, LANES), lambda b: (b, 0)),
                  pl.BlockSpec(bf.shape, const),
                  pl.BlockSpec(pm_q.shape, const), pl.BlockSpec(pm_k.shape, const),
                  pl.BlockSpec(ones_q.shape, const), pl.BlockSpec(ones_k.shape, const),
                  pl.BlockSpec(g.shape, const)],
        out_specs=pl.BlockSpec((seq, width), lambda b: (b, 0)),
        out_shape=jax.ShapeDtypeStruct((batch * seq, width), BF16),
        scratch_shapes=[pltpu.VMEM((heads, nblk, LANES, blk), BF16),
                        pltpu.VMEM((heads, nblk, blk, LANES), BF16),
                        pltpu.VMEM((heads, nblk, FOX_VT_ROWS, blk), BF16)],
        compiler_params=pltpu.CompilerParams(dimension_semantics=("arbitrary",),
                                             vmem_limit_bytes=VMEM_LIMIT_BYTES),
        name="fox",
    )(q_t, k, v_t, small, bf, pm_q, pm_k, ones_q, ones_k, g)


def _mixer_kernel(h_ref, ys_ref, yl_ref, yf_ref, p_ref, wo_ref, g2_ref, wg_ref, wu_ref, wd_ref,
                  g3_ref, wpg_ref, bpg_ref, wpp_ref, gf_ref, o_ref, *, ff_chunk, final_norm):
    y = jnp.concatenate([ys_ref[...], yl_ref[...], yf_ref[...]], axis=1)
    h = h_ref[...] + _dot(y, wo_ref[...])
    u = _rms(h, g2_ref[...]).astype(BF16)
    d_ff = wg_ref.shape[1]
    for c0 in range(0, d_ff, ff_chunk):
        gt = _dot(u, wg_ref[:, c0:c0 + ff_chunk])
        up = _dot(u, wu_ref[:, c0:c0 + ff_chunk])
        act = (_silu(gt) * up).astype(BF16)
        h = h + _dot(act, wd_ref[c0:c0 + ff_chunk, :])
    u = _rms(h, g3_ref[...]).astype(BF16)
    gate = _sigmoid(_dot(u, wpg_ref[...]) + bpg_ref[...])
    h = h + gate * _dot(p_ref[...].astype(BF16), wpp_ref[...])
    if final_norm:
        h = _rms(h, gf_ref[...])
    o_ref[...] = h


def _mixer(h, ys, yl, yf, p, wo, g2, wg, wu, wd, g3, wpg, bpg, wpp, gf, *, layer, tm, ff_chunk, final_norm):
    t, d = h.shape
    const = lambda i: (0, 0)
    row = lambda i: (i, 0)
    wspec = lambda a: pl.BlockSpec(a.shape, const, pipeline_mode=pl.Buffered(1))
    vspec = lambda a: pl.BlockSpec(a.shape, const)
    return pl.pallas_call(
        functools.partial(_mixer_kernel, ff_chunk=ff_chunk, final_norm=final_norm),
        grid=(t // tm,),
        in_specs=[pl.BlockSpec((tm, d), row),
                  pl.BlockSpec((tm, ys.shape[1]), row),
                  pl.BlockSpec((tm, yl.shape[1]), row),
                  pl.BlockSpec((tm, yf.shape[1]), row),
                  pl.BlockSpec((None, tm, p.shape[2]), lambda i: (layer, i, 0)),
                  wspec(wo), vspec(g2), wspec(wg), wspec(wu), wspec(wd),
                  vspec(g3), wspec(wpg), vspec(bpg), wspec(wpp), vspec(gf)],
        out_specs=pl.BlockSpec((tm, d), row),
        out_shape=jax.ShapeDtypeStruct((t, d), F32),
        compiler_params=pltpu.CompilerParams(dimension_semantics=("arbitrary",),
                                             vmem_limit_bytes=VMEM_LIMIT_BYTES),
        name="mixer",
    )(h, ys, yl, yf, p, wo, g2, wg, wu, wd, g3, wpg, bpg, wpp, gf)


def _pad_lanes(v, col0):
    out = jnp.zeros((1, LANES), F32)
    return out.at[0, col0:col0 + v.shape[0]].set(v.astype(F32))


def _block_diag(w):
    nb, bw, _ = w.shape
    out = jnp.zeros((nb * bw, nb * bw), w.dtype)
    for g in range(nb):
        out = out.at[g * bw:(g + 1) * bw, g * bw:(g + 1) * bw].set(w[g])
    return out


def kernel(x, p, norm1_g, w_in, ssd_conv_w, ssd_conv_b, ssd_dt_bias, ssd_a_log, ssd_d, ssd_norm_g,
           lru_conv_w, lru_conv_b, lru_w_a, lru_b_a, lru_w_x, lru_b_x, lru_lambda, lru_norm_g,
           fox_b_f, fox_norm_g, w_out, norm2_g, w_gate, w_up, w_down, norm3_g, w_ple_gate,
           b_ple_gate, w_ple_proj, final_norm_g):
    batch, seq, d_model = x.shape
    depth = w_in.shape[0]
    ssd_heads = ssd_dt_bias.shape[1]
    ssd_width = ssd_heads * HEAD_DIM
    conv_dim = ssd_conv_w.shape[2]
    ssd_state = (conv_dim - ssd_width) // (2 * SSD_GROUPS)
    lru_width = lru_conv_w.shape[2]
    fox_heads = fox_b_f.shape[1]
    fox_width = fox_heads * HEAD_DIM
    tokens = batch * seq

    o_z = 0
    o_xbc = o_z + ssd_width
    o_dt = o_xbc + conv_dim
    o_lx = o_dt + ssd_heads
    o_q = o_lx + 2 * lru_width
    o_f = o_q + 3 * fox_width
    q_gain = HEAD_DIM ** -0.5 * LOG2E
    outs = ((ssd_width, F32, 1.0, None, "silu"), (conv_dim, F32, 1.0, None, None),
            (lru_width, F32, 1.0, None, None), (lru_width, F32, 1.0, None, "gelu"),
            (fox_width, BF16, q_gain, FOX_BLOCK, None), (fox_width, BF16, 1.0, None, None),
            (fox_width, BF16, 1.0, FOX_BLOCK, None), (LANES, F32, 1.0, None, None))

    h = x.reshape(tokens, d_model)
    row = lambda v: v.reshape(1, -1).astype(F32)
    w_in_rows = w_in.reshape(depth * d_model, -1)
    p_rows = p.reshape(depth, tokens, -1)
    for i in range(depth):
        w = w_in_rows[i * d_model:(i + 1) * d_model]
        w_small = jnp.zeros((d_model, LANES), w.dtype)
        w_small = w_small.at[:, SMALL_DT_COL:SMALL_DT_COL + ssd_heads].set(w[:, o_dt:o_lx])
        w_small = w_small.at[:, SMALL_F_COL:SMALL_F_COL + fox_heads].set(w[:, o_f:o_f + fox_heads])
        w_perm = jnp.concatenate([w[:, o_z:o_dt], w[:, o_lx:o_f], w_small], axis=1).astype(BF16)

        zs, xbc, lru_x, lru_gate, q_t, k, v_t, small = _in_proj(h, row(norm1_g[i]), w_perm, outs, tm=1024)

        y_ssd = _ssd(zs, xbc, small, ssd_conv_w[i], row(ssd_conv_b[i]),
                     _pad_lanes(ssd_dt_bias[i], SMALL_DT_COL), _pad_lanes(ssd_a_log[i], SMALL_DT_COL),
                     row(jnp.repeat(ssd_d[i], HEAD_DIM)), row(ssd_norm_g[i]),
                     batch=batch, seq=seq, heads=ssd_heads, groups=SSD_GROUPS, state=ssd_state)
        y_lru = _lru(lru_x, lru_gate, lru_conv_w[i], row(lru_conv_b[i]),
                     _block_diag(lru_w_a[i]).astype(BF16), row(lru_b_a[i]),
                     _block_diag(lru_w_x[i]).astype(BF16), row(lru_b_x[i]),
                     row(lru_lambda[i]), row(lru_norm_g[i]), batch=batch, seq=seq)
        y_fox = _fox(q_t, k, v_t, small, _pad_lanes(fox_b_f[i], SMALL_F_COL), row(fox_norm_g[i]),
                     batch=batch, seq=seq, heads=fox_heads)
        h = _mixer(h, y_ssd, y_lru, y_fox, p_rows,
                   w_out[i].astype(BF16), row(norm2_g[i]),
                   w_gate[i].astype(BF16), w_up[i].astype(BF16), w_down[i].astype(BF16),
                   row(norm3_g[i]), w_ple_gate[i].astype(BF16), row(b_ple_gate[i]),
                   w_ple_proj[i].astype(BF16), row(final_norm_g),
                   layer=i, tm=512, ff_chunk=256, final_norm=(i == depth - 1))
    return h.reshape(batch, seq, d_model)
```
